```python
import jax, jax.numpy as jnp
from jax import lax
import numpy as np

D_MODEL = 1024
BATCH = 4
SEQ = 8192
DEPTH = 2

N_POOL_LAYERS = DEPTH // 2
N_ATTN_LAYERS = DEPTH - N_POOL_LAYERS

POOL_WINDOWS = (2, 4, 8, 16)
N_POOL_GROUPS = len(POOL_WINDOWS)
POOL_GROUP_WIDTH = D_MODEL // N_POOL_GROUPS

HEAD_DIM = 64
N_HEADS = D_MODEL // HEAD_DIM
Q_BLOCK = 128
ATTN_SCALE = HEAD_DIM ** -0.5
NEG_BIG = -1e30

N_GROUPS = 4
EXPERTS_PER_GROUP = 4
N_EXPERTS = N_GROUPS * EXPERTS_PER_GROUP
TOP_K = 2
D_EXPERT = D_MODEL // 4

EPS = 1e-6

kernel_name = "yoco_pool_fox_hmoe_trunk"


def rmsnorm(x, g):
    xf = x.astype(jnp.float32)
    inv = lax.rsqrt(jnp.mean(xf * xf, axis=-1, keepdims=True) + EPS)
    return (xf * inv * g.astype(jnp.float32)).astype(x.dtype)


def pool_mixer(xn, w_pool, scale):
    B, S, D = xn.shape
    xg = xn.astype(jnp.float32).reshape(B, S, N_POOL_GROUPS, POOL_GROUP_WIDTH)
    cs = jnp.cumsum(xg, axis=1)
    pos1 = jnp.arange(1, S + 1)
    outs = []
    for g, w in enumerate(POOL_WINDOWS):
        c = cs[:, :, g]
        lag = jnp.pad(c, ((0, 0), (w, 0), (0, 0)))[:, :S]
        cnt = jnp.minimum(pos1, w).astype(jnp.float32)[None, :, None]
        outs.append((c - lag) / cnt)
    pooled = jnp.stack(outs, axis=2)
    diff = (pooled - xg).astype(xn.dtype)
    y = jnp.einsum('bsgc,gcd->bsgd', diff, w_pool).reshape(B, S, D)
    return y * scale


def shared_kv(h, kv_norm, w_k, w_v, w_f, b_f):
    B, S, _ = h.shape
    hn = rmsnorm(h, kv_norm)
    k = (hn @ w_k).reshape(B, S, N_HEADS, HEAD_DIM).transpose(0, 2, 1, 3)
    v = (hn @ w_v).reshape(B, S, N_HEADS, HEAD_DIM).transpose(0, 2, 1, 3)
    log_f = jax.nn.log_sigmoid((hn @ w_f).astype(jnp.float32) + b_f.astype(jnp.float32))
    F = jnp.cumsum(log_f, axis=1).transpose(0, 2, 1)
    return k, v, F


def forgetting_attention(xn, w_q, w_o, k, v, F):
    B, S, _ = xn.shape
    n_blk = S // Q_BLOCK
    q = (xn @ w_q).reshape(B, S, N_HEADS, HEAD_DIM).transpose(0, 2, 1, 3)
    q_blocks = q.reshape(B, N_HEADS, n_blk, Q_BLOCK, HEAD_DIM).transpose(2, 0, 1, 3, 4)
    F_blocks = F.reshape(B, N_HEADS, n_blk, Q_BLOCK).transpose(2, 0, 1, 3)
    k_pos = jnp.arange(S)

    def one_block(args):
        i, qb, Fb = args
        s = jnp.einsum('bhqd,bhkd->bhqk', qb, k).astype(jnp.float32) * ATTN_SCALE
        s = s + Fb[..., None] - F[:, :, None, :]
        q_pos = i * Q_BLOCK + jnp.arange(Q_BLOCK)
        mask = k_pos[None, :] <= q_pos[:, None]
        s = jnp.where(mask, s, NEG_BIG)
        p = jax.nn.softmax(s, axis=-1).astype(v.dtype)
        return jnp.einsum('bhqk,bhkd->bhqd', p, v)

    o = lax.map(one_block, (jnp.arange(n_blk), q_blocks, F_blocks))
    o = o.transpose(1, 0, 3, 2, 4).reshape(B, S, N_HEADS * HEAD_DIM)
    return o @ w_o


def hierarchical_moe(xn, wg, bg, we, be, w_gate, w_up, w_down):
    B, S, D = xn.shape
    N = B * S
    t = xn.reshape(N, D)
    g_logits = (t @ wg).astype(jnp.float32) + bg.astype(jnp.float32)
    g_prob = jax.nn.softmax(g_logits, axis=-1)
    g_idx = jnp.argmax(g_logits, axis=-1)
    g_w = jnp.take_along_axis(g_prob, g_idx[:, None], axis=1)
    e_logits = ((t @ we).astype(jnp.float32) + be.astype(jnp.float32)).reshape(N, N_GROUPS, EXPERTS_PER_GROUP)
    e_sel = jnp.take_along_axis(e_logits, g_idx[:, None, None], axis=1)[:, 0]
    top_v, top_i = lax.top_k(e_sel, TOP_K)
    top_w = jax.nn.softmax(top_v, axis=-1) * g_w
    expert_id = g_idx[:, None] * EXPERTS_PER_GROUP + top_i
    gates = jnp.sum(jax.nn.one_hot(expert_id, N_EXPERTS, dtype=jnp.float32) * top_w[..., None], axis=1)
    hg = jnp.einsum('nd,edf->nef', t, w_gate)
    hu = jnp.einsum('nd,edf->nef', t, w_up)
    h = jax.nn.silu(hg) * hu * gates[:, :, None].astype(hu.dtype)
    y = h.reshape(N, N_EXPERTS * D_EXPERT) @ w_down.reshape(N_EXPERTS * D_EXPERT, D)
    return y.reshape(B, S, D)


def setup_inputs(seed: int = 0) -> dict:
    key = jax.random.key(seed)
    ks = jax.random.split(key, 21)
    f32 = jnp.float32
    D = D_MODEL
    HD_ALL = N_HEADS * HEAD_DIM
    GW = POOL_GROUP_WIDTH

    def nrm(k, shape, fan_in):
        return jax.random.normal(k, shape, f32) * fan_in ** -0.5

    def gain(k, shape):
        return 1.0 + 0.02 * jax.random.normal(k, shape, f32)

    return {
        "x": jax.random.normal(ks[0], (BATCH, SEQ, D), f32),
        "mix_norm": gain(ks[1], (DEPTH, D)),
        "ffn_norm": gain(ks[2], (DEPTH, D)),
        "pool_w": nrm(ks[3], (N_POOL_LAYERS, N_POOL_GROUPS, GW, GW), GW),
        "pool_scale": 0.5 + 0.05 * jax.random.normal(ks[4], (N_POOL_LAYERS, D), f32),
        "kv_norm": gain(ks[5], (D,)),
        "w_k": nrm(ks[6], (D, HD_ALL), D),
        "w_v": nrm(ks[7], (D, HD_ALL), D),
        "w_f": nrm(ks[8], (D, N_HEADS), D),
        "b_f": 2.0 + 0.5 * jax.random.normal(ks[9], (N_HEADS,), f32),
        "w_q": nrm(ks[10], (N_ATTN_LAYERS, D, HD_ALL), D),
        "w_o": nrm(ks[11], (N_ATTN_LAYERS, HD_ALL, D), HD_ALL),
        "router_g": nrm(ks[12], (DEPTH, D, N_GROUPS), D),
        "router_g_b": 0.01 * jax.random.normal(ks[13], (DEPTH, N_GROUPS), f32),
        "router_e": nrm(ks[14], (DEPTH, D, N_EXPERTS), D),
        "router_e_b": 0.01 * jax.random.normal(ks[15], (DEPTH, N_EXPERTS), f32),
        "w_gate": nrm(ks[16], (DEPTH, N_EXPERTS, D, D_EXPERT), D),
        "w_up": nrm(ks[17], (DEPTH, N_EXPERTS, D, D_EXPERT), D),
        "w_down": nrm(ks[18], (DEPTH, N_EXPERTS, D_EXPERT, D), D_EXPERT),
        "final_norm": gain(ks[19], (D,)),
    }


def reference(x, mix_norm, ffn_norm, pool_w, pool_scale, kv_norm, w_k, w_v, w_f, b_f,
              w_q, w_o, router_g, router_g_b, router_e, router_e_b,
              w_gate, w_up, w_down, final_norm):
    h = x
    k = v = F = None
    for layer in range(DEPTH):
        if layer == N_POOL_LAYERS:
            k, v, F = shared_kv(h, kv_norm, w_k, w_v, w_f, b_f)
        hn = rmsnorm(h, mix_norm[layer])
        if layer < N_POOL_LAYERS:
            h = h + pool_mixer(hn, pool_w[layer], pool_scale[layer])
        else:
            j = layer - N_POOL_LAYERS
            h = h + forgetting_attention(hn, w_q[j], w_o[j], k, v, F)
        hn = rmsnorm(h, ffn_norm[layer])
        h = h + hierarchical_moe(hn, router_g[layer], router_g_b[layer], router_e[layer],
                                 router_e_b[layer], w_gate[layer], w_up[layer], w_down[layer])
    return rmsnorm(h, final_norm)
```

```python
import functools
import math

import numpy as np
import jax
import jax.numpy as jnp
from jax import lax
from jax.experimental import pallas as pl
from jax.experimental.pallas import tpu as pltpu

POOL_WINDOWS = (2, 4, 8, 16)
HEAD_DIM = 64
N_GROUPS = 4
EXPERTS_PER_GROUP = 4
EPS = 1e-6
NEG_BIG = -1e30
LOG2E = math.log2(math.e)

LANES = 128
POOL_HALO = 16
VMEM_LIMIT = 56 * 1024 * 1024

F_PIECES = 3
AUG_PER_HEAD = 2 * F_PIECES
ONE_LANE = 3 * 16

f32 = jnp.float32
bf16 = jnp.bfloat16


def _rms_inv(x):
    return lax.rsqrt(jnp.mean(x * x, axis=-1, keepdims=True) + EPS)


def _split3(x):
    hi = x.astype(bf16)
    r1 = x - hi.astype(f32)
    mid = r1.astype(bf16)
    lo = (r1 - mid.astype(f32)).astype(bf16)
    return hi, mid, lo


def _mix0_kernel(x_ref, halo_ref, g_ref, w_ref, sc_ref, o_ref, *, tile, group_width):
    t = pl.program_id(1)
    x = x_ref[0]
    g = g_ref[...]
    xn = x * _rms_inv(x) * g
    hal = halo_ref[0]
    hn = hal * _rms_inv(hal) * g * (t > 0).astype(f32)
    ext = jnp.concatenate([hn, xn], axis=0)
    pos1 = t * tile + lax.broadcasted_iota(jnp.int32, (tile, 1), 0) + 1
    outs = []
    for gi, w in enumerate(POOL_WINDOWS):
        e = ext[:, gi * group_width:(gi + 1) * group_width]
        s = e
        sh = 1
        while sh < w:
            s = s + pltpu.roll(s, sh, axis=0)
            sh *= 2
        cnt = jnp.minimum(pos1, w).astype(f32)
        diff = (s[POOL_HALO:] / cnt - e[POOL_HALO:]).astype(bf16)
        outs.append(jnp.dot(diff, w_ref[gi], preferred_element_type=f32))
    y = jnp.concatenate(outs, axis=1) * sc_ref[...]
    o_ref[0] = x + y


def _mix0(x, gain, pool_w, pool_scale, *, tile=512):
    B, S, D = x.shape
    G = len(POOL_WINDOWS)
    gw = D // G
    assert all(w & (w - 1) == 0 and w <= POOL_HALO for w in POOL_WINDOWS)
    assert S % tile == 0 and tile % POOL_HALO == 0
    hb = tile // POOL_HALO
    return pl.pallas_call(
        functools.partial(_mix0_kernel, tile=tile, group_width=gw),
        grid=(B, S // tile),
        in_specs=[
            pl.BlockSpec((1, tile, D), lambda b, t: (b, t, 0)),
            pl.BlockSpec((1, POOL_HALO, D), lambda b, t: (b, jnp.maximum(t * hb - 1, 0), 0)),
            pl.BlockSpec((1, D), lambda b, t: (0, 0)),
            pl.BlockSpec((G, gw, gw), lambda b, t: (0, 0, 0)),
            pl.BlockSpec((1, D), lambda b, t: (0, 0)),
        ],
        out_specs=pl.BlockSpec((1, tile, D), lambda b, t: (b, t, 0)),
        out_shape=jax.ShapeDtypeStruct((B, S, D), f32),
        compiler_params=pltpu.CompilerParams(
            dimension_semantics=("parallel", "parallel"), vmem_limit_bytes=VMEM_LIMIT),
        name="mix0",
    )(x, x, gain.reshape(1, D), pool_w.astype(bf16), pool_scale.reshape(1, D))


def _route(logits):
    G, E = N_GROUPS, EXPERTS_PER_GROUP
    lane = lax.broadcasted_iota(jnp.int32, logits.shape, 1).astype(f32)
    far = float(LANES)
    is_g = lane < G
    gl = jnp.where(is_g, logits, -jnp.inf)
    gmax = jnp.max(gl, axis=-1, keepdims=True)
    gidx = jnp.min(jnp.where(gl == gmax, lane, far), axis=-1, keepdims=True)
    gsum = jnp.sum(jnp.where(is_g, jnp.exp(logits - gmax), 0.0), axis=-1, keepdims=True)
    g_w = 1.0 / gsum
    lo = G + E * gidx
    el = jnp.where((lane >= lo) & (lane < lo + E), logits, -jnp.inf)
    m1 = jnp.max(el, axis=-1, keepdims=True)
    i1 = jnp.min(jnp.where(el == m1, lane, far), axis=-1, keepdims=True)
    el2 = jnp.where(lane == i1, -jnp.inf, el)
    m2 = jnp.max(el2, axis=-1, keepdims=True)
    i2 = jnp.min(jnp.where(el2 == m2, lane, far), axis=-1, keepdims=True)
    e2 = jnp.exp(m2 - m1)
    w1 = 1.0 / (1.0 + e2)
    w2 = e2 * w1
    return jnp.where(lane == i1, w1 * g_w, jnp.where(lane == i2, w2 * g_w, 0.0))


def _moe_kernel(*refs, with_attn, with_final, d_expert):
    refs = list(refs)
    h_ref = refs.pop(0)
    if with_attn:
        a_ref = refs.pop(0)
        wo_ref = refs.pop(0)
    g_ref, wr_hi_ref, wr_lo_ref, br_ref, wgu_ref, wd_ref = refs[:6]
    refs = refs[6:]
    if with_final:
        fg_ref = refs.pop(0)
    out_ref, hn_scr, gates_scr = refs

    grp = pl.program_id(1)
    E, F = EXPERTS_PER_GROUP, d_expert

    @pl.when(grp == 0)
    def _():
        x = h_ref[...]
        if with_attn:
            x = x + jnp.dot(a_ref[...], wo_ref[...], preferred_element_type=f32)
        out_ref[...] = x
        hn = x * _rms_inv(x) * g_ref[...]
        h1 = hn.astype(bf16)
        h2 = (hn - h1.astype(f32)).astype(bf16)
        logits = (jnp.dot(h1, wr_hi_ref[...], preferred_element_type=f32)
                  + jnp.dot(h1, wr_lo_ref[...], preferred_element_type=f32)
                  + jnp.dot(h2, wr_hi_ref[...], preferred_element_type=f32)
                  + br_ref[...])
        hn_scr[...] = h1
        gates_scr[...] = _route(logits)

    hb = hn_scr[...]
    hgu = jnp.dot(hb, wgu_ref[0], preferred_element_type=f32)
    gg = hgu[:, :E * F]
    uu = hgu[:, E * F:]
    gates = gates_scr[...]
    lane = lax.broadcasted_iota(jnp.int32, gates.shape, 1)
    cols = []
    for e in range(E):
        sel = jnp.sum(jnp.where(lane == N_GROUPS + E * grp + e, gates, 0.0), axis=-1, keepdims=True)
        cols.append(jnp.broadcast_to(sel, (gates.shape[0], F)))
    gb = jnp.concatenate(cols, axis=1)
    act = (gg * (1.0 / (1.0 + jnp.exp(-gg))) * uu * gb).astype(bf16)
    out_ref[...] += jnp.dot(act, wd_ref[0], preferred_element_type=f32)

    if with_final:
        @pl.when(grp == pl.num_programs(1) - 1)
        def _():
            o = out_ref[...]
            out_ref[...] = o * _rms_inv(o) * fg_ref[...]


def _moe(h, ffn_gain, wg, bg, we, be, w_gate, w_up, w_down, *, attn=None, w_o=None,
         final_gain=None, tile=512):
    N, D = h.shape
    G, E = N_GROUPS, EXPERTS_PER_GROUP
    F = w_gate.shape[-1]
    assert N % tile == 0 and G + G * E <= LANES
    with_attn = attn is not None
    with_final = final_gain is not None

    wr = jnp.zeros((D, LANES), f32).at[:, :G].set(wg).at[:, G:G + G * E].set(we)
    wr_hi = wr.astype(bf16)
    wr_lo = (wr - wr_hi.astype(f32)).astype(bf16)
    br = jnp.zeros((1, LANES), f32).at[0, :G].set(bg).at[0, G:G + G * E].set(be)
    wgate_g = w_gate.reshape(G, E, D, F).transpose(0, 2, 1, 3).reshape(G, D, E * F)
    wup_g = w_up.reshape(G, E, D, F).transpose(0, 2, 1, 3).reshape(G, D, E * F)
    wgu = jnp.concatenate([wgate_g, wup_g], axis=-1).astype(bf16)
    wd = w_down.reshape(G, E * F, D).astype(bf16)

    tok = lambda i, g: (i, 0)
    const = lambda i, g: (0, 0)
    in_specs = [pl.BlockSpec((tile, D), tok)]
    args = [h]
    if with_attn:
        in_specs += [pl.BlockSpec((tile, D), tok), pl.BlockSpec((D, D), const)]
        args += [attn, w_o.astype(bf16)]
    in_specs += [
        pl.BlockSpec((1, D), const),
        pl.BlockSpec((D, LANES), const),
        pl.BlockSpec((D, LANES), const),
        pl.BlockSpec((1, LANES), const),
        pl.BlockSpec((1, D, 2 * E * F), lambda i, g: (g, 0, 0)),
        pl.BlockSpec((1, E * F, D), lambda i, g: (g, 0, 0)),
    ]
    args += [ffn_gain.reshape(1, D), wr_hi, wr_lo, br, wgu, wd]
    if with_final:
        in_specs.append(pl.BlockSpec((1, D), const))
        args.append(final_gain.reshape(1, D))

    return pl.pallas_call(
        functools.partial(_moe_kernel, with_attn=with_attn, with_final=with_final, d_expert=F),
        grid=(N // tile, G),
        in_specs=in_specs,
        out_specs=pl.BlockSpec((tile, D), tok),
        out_shape=jax.ShapeDtypeStruct((N, D), f32),
        scratch_shapes=[pltpu.VMEM((tile, D), bf16), pltpu.VMEM((tile, LANES), f32)],
        compiler_params=pltpu.CompilerParams(
            dimension_semantics=("parallel", "arbitrary"), vmem_limit_bytes=VMEM_LIMIT),
        name="moe_final" if with_final else "moe",
    )(*args)


def _kvq_kernel(h_ref, gkv_ref, gq_ref, wk_ref, wv_ref, wq_ref, wf_ref, bf_ref, tri_ref,
                selq_ref, selk_ref, q_ref, qa_ref, k_ref, ka_ref, v_ref, carry_scr, *, q_scale):
    t = pl.program_id(1)

    @pl.when(t == 0)
    def _():
        carry_scr[...] = jnp.zeros_like(carry_scr)

    x = h_ref[0]
    xn = x * _rms_inv(x)
    hk = (xn * gkv_ref[...]).astype(bf16)
    hq = (xn * gq_ref[...]).astype(bf16)
    k_ref[0] = jnp.dot(hk, wk_ref[...], preferred_element_type=f32).astype(bf16)
    v_ref[0] = jnp.dot(hk, wv_ref[...], preferred_element_type=f32).astype(bf16)
    q_ref[0] = (jnp.dot(hq, wq_ref[...], preferred_element_type=f32) * q_scale).astype(bf16)

    z = jnp.dot(hk, wf_ref[...], preferred_element_type=f32) + bf_ref[...]
    lane = lax.broadcasted_iota(jnp.int32, z.shape, 1)
    log_f = jnp.minimum(z, 0.0) - jnp.log(1.0 + jnp.exp(-jnp.abs(z)))
    lf2 = jnp.where(lane < ONE_LANE, log_f * LOG2E, 0.0)
    hi, mid, lo = _split3(lf2)
    tri = tri_ref[...]
    F = (jnp.dot(tri, hi, preferred_element_type=f32)
         + jnp.dot(tri, mid, preferred_element_type=f32)
         + jnp.dot(tri, lo, preferred_element_type=f32)
         + carry_scr[0:1, :])
    carry_scr[0:1, :] = F[-1:, :]
    fh, fm, fl = _split3(F)
    packed = jnp.where(lane < 16, fh, jnp.where(lane < 32, fm, jnp.where(lane < ONE_LANE, fl,
                       jnp.where(lane == ONE_LANE, 1.0, 0.0).astype(bf16))))
    qa_ref[0] = jnp.dot(packed, selq_ref[...], preferred_element_type=f32).astype(bf16)
    ka_ref[0] = jnp.dot(packed, selk_ref[...], preferred_element_type=f32).astype(bf16)


def _aug_selectors(n_heads):
    n_pairs = n_heads // 2
    selq = np.zeros((LANES, n_pairs * LANES), np.float32)
    selk = np.zeros((LANES, n_pairs * LANES), np.float32)
    for h in range(n_heads):
        base = (h // 2) * LANES + AUG_PER_HEAD * (h % 2)
        for p in range(F_PIECES):
            selq[h + 16 * p, base + p] = 1.0
            selq[ONE_LANE, base + F_PIECES + p] = 1.0
            selk[ONE_LANE, base + p] = 1.0
            selk[h + 16 * p, base + F_PIECES + p] = -1.0
    return jnp.asarray(selq, bf16), jnp.asarray(selk, bf16)


def _kvq(h, kv_gain, q_gain, w_k, w_v, w_q, w_f, b_f, *, tile=512):
    B, S, D = h.shape
    H = w_f.shape[1]
    assert H == 16 and D == H * HEAD_DIM and S % tile == 0
    wf = jnp.zeros((D, LANES), f32)
    bfp = jnp.zeros((1, LANES), f32)
    for p in range(F_PIECES):
        wf = wf.at[:, 16 * p:16 * p + H].set(w_f)
        bfp = bfp.at[0, 16 * p:16 * p + H].set(b_f)
    tri = jnp.asarray(np.tril(np.ones((tile, tile), np.float32)), bf16)
    selq, selk = _aug_selectors(H)
    q_scale = HEAD_DIM ** -0.5 * LOG2E

    const = lambda b, t: (0, 0)
    blk = pl.BlockSpec((1, tile, D), lambda b, t: (b, t, 0))
    out = jax.ShapeDtypeStruct((B, S, D), bf16)
    return pl.pallas_call(
        functools.partial(_kvq_kernel, q_scale=q_scale),
        grid=(B, S // tile),
        in_specs=[
            blk,
            pl.BlockSpec((1, D), const), pl.BlockSpec((1, D), const),
            pl.BlockSpec((D, D), const), pl.BlockSpec((D, D), const), pl.BlockSpec((D, D), const),
            pl.BlockSpec((D, LANES), const), pl.BlockSpec((1, LANES), const),
            pl.BlockSpec((tile, tile), const),
            pl.BlockSpec((LANES, D // 2 // HEAD_DIM * LANES), const),
            pl.BlockSpec((LANES, D // 2 // HEAD_DIM * LANES), const),
        ],
        out_specs=[blk] * 5,
        out_shape=[out] * 5,
        scratch_shapes=[pltpu.VMEM((8, LANES), f32)],
        compiler_params=pltpu.CompilerParams(
            dimension_semantics=("parallel", "arbitrary"), vmem_limit_bytes=VMEM_LIMIT),
        name="kvq",
    )(h, kv_gain.reshape(1, D), q_gain.reshape(1, D), w_k.astype(bf16), w_v.astype(bf16),
      w_q.astype(bf16), wf.astype(bf16), bfp, tri, selq, selk)


def _attn_kernel(q_ref, qa_ref, k_ref, ka_ref, v_ref, o_ref, kf_scr, vf_scr, acc_scr, m_scr, *, tile):
    i = pl.program_id(2)
    T = tile

    @pl.when(i == 0)
    def _():
        kf_scr[:, :LANES] = k_ref[0]
        kf_scr[:, LANES:] = ka_ref[0]
        vf_scr[:, :LANES] = v_ref[0]
        ln = lax.broadcasted_iota(jnp.int32, (vf_scr.shape[0], LANES), 1)
        vf_scr[:, LANES:] = jnp.where(ln == 0, 1.0, 0.0).astype(bf16)

    lane = lax.broadcasted_iota(jnp.int32, (T, LANES), 1)
    row = lax.broadcasted_iota(jnp.int32, (T, T), 0)
    col = lax.broadcasted_iota(jnp.int32, (T, T), 1)
    zero = jnp.zeros((T, LANES), bf16)
    n_rep = T // LANES
    outs = []
    for hh in range(2):
        qm = jnp.where((lane >= HEAD_DIM * hh) & (lane < HEAD_DIM * (hh + 1)), q_ref[0], zero)
        qam = jnp.where((lane >= AUG_PER_HEAD * hh) & (lane < AUG_PER_HEAD * (hh + 1)), qa_ref[0], zero)
        qf = jnp.concatenate([qm, qam], axis=1)
        m_scr[...] = jnp.full(m_scr.shape, NEG_BIG, f32)
        acc_scr[...] = jnp.zeros(acc_scr.shape, f32)

        def step(j, masked, qf=qf):
            kb = kf_scr[pl.ds(pl.multiple_of(j * T, T), T), :]
            s = lax.dot_general(qf, kb, (((1,), (1,)), ((), ())), preferred_element_type=f32)
            if masked:
                s = jnp.where(col <= row, s, NEG_BIG)
            m_prev = m_scr[...]
            m_new = jnp.maximum(m_prev, jnp.max(s, axis=-1, keepdims=True))
            alpha = jnp.exp2(m_prev - m_new)
            p = jnp.exp2(s - jnp.concatenate([m_new] * n_rep, axis=1)).astype(bf16)
            pv = jnp.dot(p, vf_scr[pl.ds(pl.multiple_of(j * T, T), T), :], preferred_element_type=f32)
            acc_scr[...] = acc_scr[...] * jnp.concatenate([alpha, alpha], axis=1) + pv
            m_scr[...] = m_new

        def body(j, c):
            step(j, False)
            return c

        lax.fori_loop(0, i, body, 0)
        step(i, True)
        acc = acc_scr[...]
        outs.append(acc[:, :LANES] / acc[:, LANES:LANES + 1])
    o_ref[0] = jnp.where(lane < HEAD_DIM, outs[0], outs[1]).astype(bf16)


def _attn(q, qa, k, ka, v, *, tile=512):
    B, S, D = q.shape
    n_pairs = D // LANES
    assert S % tile == 0 and tile % LANES == 0
    qblk = pl.BlockSpec((1, tile, LANES), lambda b, p, i: (b, i, p))
    kblk = pl.BlockSpec((1, S, LANES), lambda b, p, i: (b, 0, p))
    return pl.pallas_call(
        functools.partial(_attn_kernel, tile=tile),
        grid=(B, n_pairs, S // tile),
        in_specs=[qblk, qblk, kblk, kblk, kblk],
        out_specs=qblk,
        out_shape=jax.ShapeDtypeStruct((B, S, D), bf16),
        scratch_shapes=[
            pltpu.VMEM((S, 2 * LANES), bf16),
            pltpu.VMEM((S, 2 * LANES), bf16),
            pltpu.VMEM((tile, 2 * LANES), f32),
            pltpu.VMEM((tile, LANES), f32),
        ],
        compiler_params=pltpu.CompilerParams(
            dimension_semantics=("parallel", "parallel", "arbitrary"), vmem_limit_bytes=VMEM_LIMIT),
        name="attn",
    )(q, qa, k, ka, v)


def kernel(x, mix_norm, ffn_norm, pool_w, pool_scale, kv_norm, w_k, w_v, w_f, b_f, w_q, w_o,
           router_g, router_g_b, router_e, router_e_b, w_gate, w_up, w_down, final_norm):
    B, S, D = x.shape
    assert mix_norm.shape[0] == 2 and pool_w.shape[0] == 1 and w_q.shape[0] == 1

    def moe(h, layer, **kw):
        return _moe(h, ffn_norm[layer], router_g[layer], router_g_b[layer], router_e[layer],
                    router_e_b[layer], w_gate[layer], w_up[layer], w_down[layer], **kw)

    h = _mix0(x, mix_norm[0], pool_w[0], pool_scale[0])
    h = moe(h.reshape(B * S, D), 0)
    q, qa, k, ka, v = _kvq(h.reshape(B, S, D), kv_norm, mix_norm[1], w_k, w_v, w_q[0], w_f, b_f)
    o = _attn(q, qa, k, ka, v)
    out = moe(h, 1, attn=o.reshape(B * S, D), w_o=w_o[0], final_gain=final_norm)
    return out.reshape(B, S, D)
```

```python
import functools
import math

import numpy as np
import jax
import jax.numpy as jnp
from jax import lax
from jax.experimental import pallas as pl
from jax.experimental.pallas import tpu as pltpu

POOL_WINDOWS = (2, 4, 8, 16)
HEAD_DIM = 64
N_GROUPS = 4
EXPERTS_PER_GROUP = 4
EPS = 1e-6
NEG_BIG = -1e30
LOG2E = math.log2(math.e)

LANES = 128
POOL_HALO = 16
VMEM_LIMIT = 56 * 1024 * 1024

F_PIECES = 3
AUG_PER_HEAD = 2 * F_PIECES
ONE_LANE = 3 * 16

f32 = jnp.float32
bf16 = jnp.bfloat16


def _rms_inv(x):
    return lax.rsqrt(jnp.mean(x * x, axis=-1, keepdims=True) + EPS)


def _split3(x):
    hi = x.astype(bf16)
    r1 = x - hi.astype(f32)
    mid = r1.astype(bf16)
    lo = (r1 - mid.astype(f32)).astype(bf16)
    return hi, mid, lo


def _mix0_kernel(x_ref, halo_ref, g_ref, w_ref, sc_ref, o_ref, *, tile, group_width):
    t = pl.program_id(1)
    x = x_ref[0]
    g = g_ref[...]
    xn = x * _rms_inv(x) * g
    hal = halo_ref[0]
    hn = hal * _rms_inv(hal) * g * (t > 0).astype(f32)
    ext = jnp.concatenate([hn, xn], axis=0)
    pos1 = t * tile + lax.broadcasted_iota(jnp.int32, (tile, 1), 0) + 1
    outs = []
    for gi, w in enumerate(POOL_WINDOWS):
        e = ext[:, gi * group_width:(gi + 1) * group_width]
        s = e
        sh = 1
        while sh < w:
            s = s + pltpu.roll(s, sh, axis=0)
            sh *= 2
        cnt = jnp.minimum(pos1, w).astype(f32)
        diff = (s[POOL_HALO:] / cnt - e[POOL_HALO:]).astype(bf16)
        outs.append(jnp.dot(diff, w_ref[gi], preferred_element_type=f32))
    y = jnp.concatenate(outs, axis=1) * sc_ref[...]
    o_ref[0] = x + y


def _mix0(x, gain, pool_w, pool_scale, *, tile=512):
    B, S, D = x.shape
    G = len(POOL_WINDOWS)
    gw = D // G
    assert all(w & (w - 1) == 0 and w <= POOL_HALO for w in POOL_WINDOWS)
    assert S % tile == 0 and tile % POOL_HALO == 0
    hb = tile // POOL_HALO
    return pl.pallas_call(
        functools.partial(_mix0_kernel, tile=tile, group_width=gw),
        grid=(B, S // tile),
        in_specs=[
            pl.BlockSpec((1, tile, D), lambda b, t: (b, t, 0)),
            pl.BlockSpec((1, POOL_HALO, D), lambda b, t: (b, jnp.maximum(t * hb - 1, 0), 0)),
            pl.BlockSpec((1, D), lambda b, t: (0, 0)),
            pl.BlockSpec((G, gw, gw), lambda b, t: (0, 0, 0)),
            pl.BlockSpec((1, D), lambda b, t: (0, 0)),
        ],
        out_specs=pl.BlockSpec((1, tile, D), lambda b, t: (b, t, 0)),
        out_shape=jax.ShapeDtypeStruct((B, S, D), f32),
        compiler_params=pltpu.CompilerParams(
            dimension_semantics=("parallel", "parallel"), vmem_limit_bytes=VMEM_LIMIT),
        name="mix0",
    )(x, x, gain.reshape(1, D), pool_w.astype(bf16), pool_scale.reshape(1, D))


def _route(logits):
    G, E = N_GROUPS, EXPERTS_PER_GROUP
    lane = lax.broadcasted_iota(jnp.int32, logits.shape, 1).astype(f32)
    far = float(LANES)
    is_g = lane < G
    gl = jnp.where(is_g, logits, -jnp.inf)
    gmax = jnp.max(gl, axis=-1, keepdims=True)
    gidx = jnp.min(jnp.where(gl == gmax, lane, far), axis=-1, keepdims=True)
    gsum = jnp.sum(jnp.where(is_g, jnp.exp(logits - gmax), 0.0), axis=-1, keepdims=True)
    g_w = 1.0 / gsum
    lo = G + E * gidx
    el = jnp.where((lane >= lo) & (lane < lo + E), logits, -jnp.inf)
    m1 = jnp.max(el, axis=-1, keepdims=True)
    i1 = jnp.min(jnp.where(el == m1, lane, far), axis=-1, keepdims=True)
    el2 = jnp.where(lane == i1, -jnp.inf, el)
    m2 = jnp.max(el2, axis=-1, keepdims=True)
    i2 = jnp.min(jnp.where(el2 == m2, lane, far), axis=-1, keepdims=True)
    e2 = jnp.exp(m2 - m1)
    w1 = 1.0 / (1.0 + e2)
    w2 = e2 * w1
    return jnp.where(lane == i1, w1 * g_w, jnp.where(lane == i2, w2 * g_w, 0.0))


def _moe_kernel(*refs, with_attn, with_final, d_expert):
    refs = list(refs)
    h_ref = refs.pop(0)
    if with_attn:
        a_ref = refs.pop(0)
        wo_ref = refs.pop(0)
    g_ref, wr_hi_ref, wr_lo_ref, br_ref, wgu_ref, wd_ref = refs[:6]
    refs = refs[6:]
    if with_final:
        fg_ref = refs.pop(0)
    out_ref, hn_scr, gates_scr = refs

    grp = pl.program_id(1)
    E, F = EXPERTS_PER_GROUP, d_expert

    @pl.when(grp == 0)
    def _():
        x = h_ref[...]
        if with_attn:
            x = x + jnp.dot(a_ref[...], wo_ref[...], preferred_element_type=f32)
        out_ref[...] = x
        hn = x * _rms_inv(x) * g_ref[...]
        h1 = hn.astype(bf16)
        h2 = (hn - h1.astype(f32)).astype(bf16)
        logits = (jnp.dot(h1, wr_hi_ref[...], preferred_element_type=f32)
                  + jnp.dot(h1, wr_lo_ref[...], preferred_element_type=f32)
                  + jnp.dot(h2, wr_hi_ref[...], preferred_element_type=f32)
                  + br_ref[...])
        hn_scr[...] = h1
        gates_scr[...] = _route(logits)

    hb = hn_scr[...]
    hgu = jnp.dot(hb, wgu_ref[0], preferred_element_type=f32)
    gg = hgu[:, :E * F]
    uu = hgu[:, E * F:]
    gates = gates_scr[...]
    lane = lax.broadcasted_iota(jnp.int32, gates.shape, 1)
    cols = []
    for e in range(E):
        sel = jnp.sum(jnp.where(lane == N_GROUPS + E * grp + e, gates, 0.0), axis=-1, keepdims=True)
        cols.append(jnp.broadcast_to(sel, (gates.shape[0], F)))
    gb = jnp.concatenate(cols, axis=1)
    act = (gg * (1.0 / (1.0 + jnp.exp(-gg))) * uu * gb).astype(bf16)
    out_ref[...] += jnp.dot(act, wd_ref[0], preferred_element_type=f32)

    if with_final:
        @pl.when(grp == pl.num_programs(1) - 1)
        def _():
            o = out_ref[...]
            out_ref[...] = o * _rms_inv(o) * fg_ref[...]


def _moe(h, ffn_gain, wg, bg, we, be, w_gate, w_up, w_down, *, attn=None, w_o=None,
         final_gain=None, tile=512):
    N, D = h.shape
    G, E = N_GROUPS, EXPERTS_PER_GROUP
    F = w_gate.shape[-1]
    assert N % tile == 0 and G + G * E <= LANES
    with_attn = attn is not None
    with_final = final_gain is not None

    wr = jnp.zeros((D, LANES), f32).at[:, :G].set(wg).at[:, G:G + G * E].set(we)
    wr_hi = wr.astype(bf16)
    wr_lo = (wr - wr_hi.astype(f32)).astype(bf16)
    br = jnp.zeros((1, LANES), f32).at[0, :G].set(bg).at[0, G:G + G * E].set(be)
    wgate_g = w_gate.reshape(G, E, D, F).transpose(0, 2, 1, 3).reshape(G, D, E * F)
    wup_g = w_up.reshape(G, E, D, F).transpose(0, 2, 1, 3).reshape(G, D, E * F)
    wgu = jnp.concatenate([wgate_g, wup_g], axis=-1).astype(bf16)
    wd = w_down.reshape(G, E * F, D).astype(bf16)

    tok = lambda i, g: (i, 0)
    const = lambda i, g: (0, 0)
    in_specs = [pl.BlockSpec((tile, D), tok)]
    args = [h]
    if with_attn:
        in_specs += [pl.BlockSpec((tile, D), tok), pl.BlockSpec((D, D), const)]
        args += [attn, w_o.astype(bf16)]
    in_specs += [
        pl.BlockSpec((1, D), const),
        pl.BlockSpec((D, LANES), const),
        pl.BlockSpec((D, LANES), const),
        pl.BlockSpec((1, LANES), const),
        pl.BlockSpec((1, D, 2 * E * F), lambda i, g: (g, 0, 0)),
        pl.BlockSpec((1, E * F, D), lambda i, g: (g, 0, 0)),
    ]
    args += [ffn_gain.reshape(1, D), wr_hi, wr_lo, br, wgu, wd]
    if with_final:
        in_specs.append(pl.BlockSpec((1, D), const))
        args.append(final_gain.reshape(1, D))

    return pl.pallas_call(
        functools.partial(_moe_kernel, with_attn=with_attn, with_final=with_final, d_expert=F),
        grid=(N // tile, G),
        in_specs=in_specs,
        out_specs=pl.BlockSpec((tile, D), tok),
        out_shape=jax.ShapeDtypeStruct((N, D), f32),
        scratch_shapes=[pltpu.VMEM((tile, D), bf16), pltpu.VMEM((tile, LANES), f32)],
        compiler_params=pltpu.CompilerParams(
            dimension_semantics=("parallel", "arbitrary"), vmem_limit_bytes=VMEM_LIMIT),
        name="moe_final" if with_final else "moe",
    )(*args)


def _kvq_kernel(h_ref, gkv_ref, gq_ref, wk_ref, wv_ref, wq_ref, wf_ref, bf_ref, tri_ref,
                selq_ref, selk_ref, q_ref, qa_ref, k_ref, ka_ref, vt_ref, carry_scr, *, q_scale):
    t = pl.program_id(1)

    @pl.when(t == 0)
    def _():
        carry_scr[...] = jnp.zeros_like(carry_scr)

    x = h_ref[0]
    xn = x * _rms_inv(x)
    hk = (xn * gkv_ref[...]).astype(bf16)
    hq = (xn * gq_ref[...]).astype(bf16)
    k_ref[0] = jnp.dot(hk, wk_ref[...], preferred_element_type=f32).astype(bf16)
    vt_ref[0] = lax.dot_general(wv_ref[...], hk, (((1,), (1,)), ((), ())),
                                preferred_element_type=f32).astype(bf16)
    q_ref[0] = (jnp.dot(hq, wq_ref[...], preferred_element_type=f32) * q_scale).astype(bf16)

    z = jnp.dot(hk, wf_ref[...], preferred_element_type=f32) + bf_ref[...]
    lane = lax.broadcasted_iota(jnp.int32, z.shape, 1)
    log_f = jnp.minimum(z, 0.0) - jnp.log(1.0 + jnp.exp(-jnp.abs(z)))
    lf2 = jnp.where(lane < ONE_LANE, log_f * LOG2E, 0.0)
    hi, mid, lo = _split3(lf2)
    tri = tri_ref[...]
    F = (jnp.dot(tri, hi, preferred_element_type=f32)
         + jnp.dot(tri, mid, preferred_element_type=f32)
         + jnp.dot(tri, lo, preferred_element_type=f32)
         + carry_scr[0:1, :])
    carry_scr[0:1, :] = F[-1:, :]
    fh, fm, fl = _split3(F)
    packed = jnp.where(lane < 16, fh, jnp.where(lane < 32, fm, jnp.where(lane < ONE_LANE, fl,
                       jnp.where(lane == ONE_LANE, 1.0, 0.0).astype(bf16))))
    qa_ref[0] = jnp.dot(packed, selq_ref[...], preferred_element_type=f32).astype(bf16)
    ka_ref[0] = jnp.dot(packed, selk_ref[...], preferred_element_type=f32).astype(bf16)


def _aug_selectors(n_heads):
    n_pairs = n_heads // 2
    selq = np.zeros((LANES, n_pairs * LANES), np.float32)
    selk = np.zeros((LANES, n_pairs * LANES), np.float32)
    for h in range(n_heads):
        base = (h // 2) * LANES + AUG_PER_HEAD * (h % 2)
        for p in range(F_PIECES):
            selq[h + 16 * p, base + p] = 1.0
            selq[ONE_LANE, base + F_PIECES + p] = 1.0
            selk[ONE_LANE, base + p] = 1.0
            selk[h + 16 * p, base + F_PIECES + p] = -1.0
    return jnp.asarray(selq, bf16), jnp.asarray(selk, bf16)


def _kvq(h, kv_gain, q_gain, w_k, w_v, w_q, w_f, b_f, *, tile=512):
    B, S, D = h.shape
    H = w_f.shape[1]
    assert H == 16 and D == H * HEAD_DIM and S % tile == 0
    wf = jnp.zeros((D, LANES), f32)
    bfp = jnp.zeros((1, LANES), f32)
    for p in range(F_PIECES):
        wf = wf.at[:, 16 * p:16 * p + H].set(w_f)
        bfp = bfp.at[0, 16 * p:16 * p + H].set(b_f)
    tri = jnp.asarray(np.tril(np.ones((tile, tile), np.float32)), bf16)
    selq, selk = _aug_selectors(H)
    q_scale = HEAD_DIM ** -0.5 * LOG2E

    const = lambda b, t: (0, 0)
    blk = pl.BlockSpec((1, tile, D), lambda b, t: (b, t, 0))
    out = jax.ShapeDtypeStruct((B, S, D), bf16)
    return pl.pallas_call(
        functools.partial(_kvq_kernel, q_scale=q_scale),
        grid=(B, S // tile),
        in_specs=[
            blk,
            pl.BlockSpec((1, D), const), pl.BlockSpec((1, D), const),
            pl.BlockSpec((D, D), const), pl.BlockSpec((D, D), const), pl.BlockSpec((D, D), const),
            pl.BlockSpec((D, LANES), const), pl.BlockSpec((1, LANES), const),
            pl.BlockSpec((tile, tile), const),
            pl.BlockSpec((LANES, D // 2 // HEAD_DIM * LANES), const),
            pl.BlockSpec((LANES, D // 2 // HEAD_DIM * LANES), const),
        ],
        out_specs=[blk] * 4 + [pl.BlockSpec((1, D, tile), lambda b, t: (b, 0, t))],
        out_shape=[out] * 4 + [jax.ShapeDtypeStruct((B, D, S), bf16)],
        scratch_shapes=[pltpu.VMEM((8, LANES), f32)],
        compiler_params=pltpu.CompilerParams(
            dimension_semantics=("parallel", "arbitrary"), vmem_limit_bytes=VMEM_LIMIT),
        name="kvq",
    )(h, kv_gain.reshape(1, D), q_gain.reshape(1, D), w_k.astype(bf16), w_v.T.astype(bf16),
      w_q.astype(bf16), wf.astype(bf16), bfp, tri, selq, selk)


V_ROWS = HEAD_DIM + 16
Q_CHUNK = 256


def _attn_kernel(q_ref, qa_ref, k_ref, ka_ref, vt_ref, o_ref, kf_scr, vt_scr, acc_scr, m_scr,
                 st0_scr, mx0_scr, st1_scr, mx1_scr, *, tile):
    i = pl.program_id(2)
    T = tile
    n_chunks = T // Q_CHUNK
    S = kf_scr.shape[0]

    @pl.when(i == 0)
    def _():
        kf_scr[:, :LANES] = k_ref[0]
        kf_scr[:, LANES:] = ka_ref[0]
        r = lax.broadcasted_iota(jnp.int32, (V_ROWS - HEAD_DIM, S), 0)
        ones_rows = jnp.where(r == 0, 1.0, 0.0).astype(bf16)
        for hh in range(2):
            vt_scr[hh, :HEAD_DIM, :] = vt_ref[0, hh * HEAD_DIM:(hh + 1) * HEAD_DIM, :]
            vt_scr[hh, HEAD_DIM:, :] = ones_rows

    lane = lax.broadcasted_iota(jnp.int32, (T, LANES), 1)
    zero = jnp.zeros((T, LANES), bf16)
    key = lax.broadcasted_iota(jnp.int32, (T, Q_CHUNK), 0)
    qry = lax.broadcasted_iota(jnp.int32, (T, Q_CHUNK), 1)
    streams = []
    for hh in range(2):
        qm = jnp.where((lane >= HEAD_DIM * hh) & (lane < HEAD_DIM * (hh + 1)), q_ref[0], zero)
        qam = jnp.where((lane >= AUG_PER_HEAD * hh) & (lane < AUG_PER_HEAD * (hh + 1)), qa_ref[0], zero)
        qf = jnp.concatenate([qm, qam], axis=1)
        for c in range(n_chunks):
            streams.append((hh, c, qf[c * Q_CHUNK:(c + 1) * Q_CHUNK, :]))
    m_scr[...] = jnp.full(m_scr.shape, NEG_BIG, f32)
    acc_scr[...] = jnp.zeros(acc_scr.shape, f32)

    def scores(j, st_buf, mx_buf):
        kb = kf_scr[pl.ds(pl.multiple_of(j * T, T), T), :]
        for n, (_, _, qc) in enumerate(streams):
            st = lax.dot_general(kb, qc, (((1,), (1,)), ((), ())), preferred_element_type=f32)
            st_buf[n] = st
            mx_buf[n] = jnp.max(st, axis=0, keepdims=True)

    def softmax_pv(j, st_buf, mx_buf, masked):
        off = pl.multiple_of(j * T, T)
        pts, alphas = [], []
        for n, (hh, c, _) in enumerate(streams):
            st = st_buf[n]
            if masked:
                st = jnp.where(key <= qry + c * Q_CHUNK, st, NEG_BIG)
                mx = jnp.max(st, axis=0, keepdims=True)
            else:
                mx = mx_buf[n]
            m_prev = m_scr[n]
            m_new = jnp.maximum(m_prev, mx)
            alphas.append(jnp.exp2(m_prev - m_new))
            pts.append(jnp.exp2(st - m_new).astype(bf16))
            m_scr[n] = m_new
        for n, (hh, c, _) in enumerate(streams):
            pv = jnp.dot(vt_scr[hh, :, pl.ds(off, T)], pts[n], preferred_element_type=f32)
            acc_scr[n] = acc_scr[n] * alphas[n] + pv

    buf0 = (st0_scr, mx0_scr)
    buf1 = (st1_scr, mx1_scr)

    def step(j, cur, nxt, masked):
        if nxt is not None:
            scores(j + 1, *nxt)
        softmax_pv(j, *cur, masked)

    scores(0, *buf0)

    def body(jj, carry):
        step(2 * jj, buf0, buf1, False)
        step(2 * jj + 1, buf1, buf0, False)
        return carry

    lax.fori_loop(0, i // 2, body, 0)

    @pl.when(i % 2 == 0)
    def _():
        step(i, buf0, None, True)

    @pl.when(i % 2 == 1)
    def _():
        step(i - 1, buf0, buf1, False)
        step(i, buf1, None, True)

    for c in range(n_chunks):
        ot = []
        for hh in range(2):
            acc = acc_scr[hh * n_chunks + c]
            ot.append(acc[:HEAD_DIM] / acc[HEAD_DIM:HEAD_DIM + 1])
        o_ref[0, c * Q_CHUNK:(c + 1) * Q_CHUNK, :] = jnp.concatenate(ot, axis=0).T.astype(bf16)


def _attn(q, qa, k, ka, vt, *, tile=512):
    B, S, D = q.shape
    n_pairs = D // LANES
    assert S % tile == 0 and tile % Q_CHUNK == 0 and LANES == 2 * HEAD_DIM
    n_streams = 2 * (tile // Q_CHUNK)
    qblk = pl.BlockSpec((1, tile, LANES), lambda b, p, i: (b, i, p))
    kblk = pl.BlockSpec((1, S, LANES), lambda b, p, i: (b, 0, p))
    return pl.pallas_call(
        functools.partial(_attn_kernel, tile=tile),
        grid=(B, n_pairs, S // tile),
        in_specs=[qblk, qblk, kblk, kblk, pl.BlockSpec((1, LANES, S), lambda b, p, i: (b, p, 0))],
        out_specs=qblk,
        out_shape=jax.ShapeDtypeStruct((B, S, D), bf16),
        scratch_shapes=[
            pltpu.VMEM((S, 2 * LANES), bf16),
            pltpu.VMEM((2, V_ROWS, S), bf16),
            pltpu.VMEM((n_streams, V_ROWS, Q_CHUNK), f32),
            pltpu.VMEM((n_streams, 1, Q_CHUNK), f32),
        ] + 2 * [pltpu.VMEM((n_streams, tile, Q_CHUNK), f32), pltpu.VMEM((n_streams, 1, Q_CHUNK), f32)],
        compiler_params=pltpu.CompilerParams(
            dimension_semantics=("parallel", "parallel", "arbitrary"), vmem_limit_bytes=VMEM_LIMIT),
        name="attn",
    )(q, qa, k, ka, vt)


def kernel(x, mix_norm, ffn_norm, pool_w, pool_scale, kv_norm, w_k, w_v, w_f, b_f, w_q, w_o,
           router_g, router_g_b, router_e, router_e_b, w_gate, w_up, w_down, final_norm):
    B, S, D = x.shape
    assert mix_norm.shape[0] == 2 and pool_w.shape[0] == 1 and w_q.shape[0] == 1

    def moe(h, layer, **kw):
        return _moe(h, ffn_norm[layer], router_g[layer], router_g_b[layer], router_e[layer],
                    router_e_b[layer], w_gate[layer], w_up[layer], w_down[layer], **kw)

    h = _mix0(x, mix_norm[0], pool_w[0], pool_scale[0])
    h = moe(h.reshape(B * S, D), 0)
    q, qa, k, ka, v = _kvq(h.reshape(B, S, D), kv_norm, mix_norm[1], w_k, w_v, w_q[0], w_f, b_f)
    o = _attn(q, qa, k, ka, v)
    out = moe(h, 1, attn=o.reshape(B * S, D), w_o=w_o[0], final_gain=final_norm)
    return out.reshape(B, S, D)
```

```python
import functools
import math

import numpy as np
import jax
import jax.numpy as jnp
from jax import lax
from jax.experimental import pallas as pl
from jax.experimental.pallas import tpu as pltpu

POOL_WINDOWS = (2, 4, 8, 16)
HEAD_DIM = 64
N_GROUPS = 4
EXPERTS_PER_GROUP = 4
EPS = 1e-6
NEG_BIG = -1e30
LOG2E = math.log2(math.e)

LANES = 128
POOL_HALO = 16
VMEM_LIMIT = 56 * 1024 * 1024

F_PIECES = 3
AUG_PER_HEAD = 2 * F_PIECES
ONE_LANE = 3 * 16

f32 = jnp.float32
bf16 = jnp.bfloat16


def _rms_inv(x):
    return lax.rsqrt(jnp.mean(x * x, axis=-1, keepdims=True) + EPS)


def _split3(x):
    hi = x.astype(bf16)
    r1 = x - hi.astype(f32)
    mid = r1.astype(bf16)
    lo = (r1 - mid.astype(f32)).astype(bf16)
    return hi, mid, lo


def _mix0_kernel(x_ref, halo_ref, g_ref, w_ref, sc_ref, o_ref, *, tile, group_width):
    t = pl.program_id(1)
    x = x_ref[0]
    g = g_ref[...]
    xn = x * _rms_inv(x) * g
    hal = halo_ref[0]
    hn = hal * _rms_inv(hal) * g * (t > 0).astype(f32)
    ext = jnp.concatenate([hn, xn], axis=0)
    pos1 = t * tile + lax.broadcasted_iota(jnp.int32, (tile, 1), 0) + 1
    outs = []
    for gi, w in enumerate(POOL_WINDOWS):
        e = ext[:, gi * group_width:(gi + 1) * group_width]
        s = e
        sh = 1
        while sh < w:
            s = s + pltpu.roll(s, sh, axis=0)
            sh *= 2
        cnt = jnp.minimum(pos1, w).astype(f32)
        diff = (s[POOL_HALO:] / cnt - e[POOL_HALO:]).astype(bf16)
        outs.append(jnp.dot(diff, w_ref[gi], preferred_element_type=f32))
    y = jnp.concatenate(outs, axis=1) * sc_ref[...]
    o_ref[0] = x + y


def _mix0(x, gain, pool_w, pool_scale, *, tile=512):
    B, S, D = x.shape
    G = len(POOL_WINDOWS)
    gw = D // G
    assert all(w & (w - 1) == 0 and w <= POOL_HALO for w in POOL_WINDOWS)
    assert S % tile == 0 and tile % POOL_HALO == 0
    hb = tile // POOL_HALO
    return pl.pallas_call(
        functools.partial(_mix0_kernel, tile=tile, group_width=gw),
        grid=(B, S // tile),
        in_specs=[
            pl.BlockSpec((1, tile, D), lambda b, t: (b, t, 0)),
            pl.BlockSpec((1, POOL_HALO, D), lambda b, t: (b, jnp.maximum(t * hb - 1, 0), 0)),
            pl.BlockSpec((1, D), lambda b, t: (0, 0)),
            pl.BlockSpec((G, gw, gw), lambda b, t: (0, 0, 0)),
            pl.BlockSpec((1, D), lambda b, t: (0, 0)),
        ],
        out_specs=pl.BlockSpec((1, tile, D), lambda b, t: (b, t, 0)),
        out_shape=jax.ShapeDtypeStruct((B, S, D), f32),
        compiler_params=pltpu.CompilerParams(
            dimension_semantics=("parallel", "parallel"), vmem_limit_bytes=VMEM_LIMIT),
        name="mix0",
    )(x, x, gain.reshape(1, D), pool_w.astype(bf16), pool_scale.reshape(1, D))


def _route(logits):
    G, E = N_GROUPS, EXPERTS_PER_GROUP
    lane = lax.broadcasted_iota(jnp.int32, logits.shape, 1).astype(f32)
    far = float(LANES)
    is_g = lane < G
    gl = jnp.where(is_g, logits, -jnp.inf)
    gmax = jnp.max(gl, axis=-1, keepdims=True)
    gidx = jnp.min(jnp.where(gl == gmax, lane, far), axis=-1, keepdims=True)
    gsum = jnp.sum(jnp.where(is_g, jnp.exp(logits - gmax), 0.0), axis=-1, keepdims=True)
    g_w = 1.0 / gsum
    lo = G + E * gidx
    el = jnp.where((lane >= lo) & (lane < lo + E), logits, -jnp.inf)
    m1 = jnp.max(el, axis=-1, keepdims=True)
    i1 = jnp.min(jnp.where(el == m1, lane, far), axis=-1, keepdims=True)
    el2 = jnp.where(lane == i1, -jnp.inf, el)
    m2 = jnp.max(el2, axis=-1, keepdims=True)
    i2 = jnp.min(jnp.where(el2 == m2, lane, far), axis=-1, keepdims=True)
    e2 = jnp.exp(m2 - m1)
    w1 = 1.0 / (1.0 + e2)
    w2 = e2 * w1
    gates = jnp.where(lane == i1, w1 * g_w, jnp.where(lane == i2, w2 * g_w, 0.0))
    return gates, gidx


def _experts(xb, gate_cols, wg_ref, wu_ref, wd_ref, grp):
    G, E = N_GROUPS, EXPERTS_PER_GROUP
    y = None
    for e in range(E):
        eg = E * grp + e
        hg = jnp.dot(xb, wg_ref[eg], preferred_element_type=f32)
        hu = jnp.dot(xb, wu_ref[eg], preferred_element_type=f32)
        act = (hg * (1.0 / (1.0 + jnp.exp(-hg))) * hu * gate_cols[:, G + eg:G + eg + 1]).astype(bf16)
        ye = jnp.dot(act, wd_ref[eg], preferred_element_type=f32)
        y = ye if y is None else y + ye
    return y


def _moe_kernel(*refs, with_attn, with_final, cap):
    refs = list(refs)
    h_ref = refs.pop(0)
    if with_attn:
        a_ref = refs.pop(0)
        wo_ref = refs.pop(0)
    g_ref, wr_hi_ref, wr_lo_ref, br_ref, ltri_ref, wg_ref, wu_ref, wd_ref = refs[:8]
    refs = refs[8:]
    if with_final:
        fg_ref = refs.pop(0)
    out_ref, xs_scr, gs_scr, ys_scr = refs
    T = h_ref.shape[0]
    G = N_GROUPS

    x = h_ref[...]
    if with_attn:
        x = x + jnp.dot(a_ref[...], wo_ref[...], preferred_element_type=f32)
    out_ref[...] = x
    hn = x * _rms_inv(x) * g_ref[...]
    h1 = hn.astype(bf16)
    h2 = (hn - h1.astype(f32)).astype(bf16)
    logits = (jnp.dot(h1, wr_hi_ref[...], preferred_element_type=f32)
              + jnp.dot(h1, wr_lo_ref[...], preferred_element_type=f32)
              + jnp.dot(h2, wr_hi_ref[...], preferred_element_type=f32)
              + br_ref[...])
    gates, gidx = _route(logits)

    lane = lax.broadcasted_iota(jnp.int32, (T, LANES), 1).astype(f32)
    onehot = jnp.where(lane == gidx, 1.0, 0.0)
    before = jnp.dot(ltri_ref[...], onehot.astype(bf16), preferred_element_type=f32)
    rank = jnp.sum(before * onehot, axis=-1, keepdims=True)
    counts = before[T - 1:T, :] + onehot[T - 1:T, :]

    fits = jnp.sum(jnp.where(counts <= cap, onehot, 0.0), axis=-1, keepdims=True)
    slot = jnp.where(fits > 0.0, gidx * cap + rank, -1.0)
    sel = jnp.where(slot == lax.broadcasted_iota(jnp.int32, (T, G * cap), 1).astype(f32),
                    1.0, 0.0).astype(bf16)
    g_hi = gates.astype(bf16)
    g_lo = (gates - g_hi.astype(f32)).astype(bf16)
    tn = (((0,), (0,)), ((), ()))
    xs_scr[...] = lax.dot_general(sel, h1, tn, preferred_element_type=f32).astype(bf16)
    gs2 = lax.dot_general(sel, jnp.concatenate([g_hi, g_lo], axis=1), tn, preferred_element_type=f32)
    gs_scr[...] = gs2[:, :LANES] + gs2[:, LANES:]

    for grp in range(G):
        count = jnp.sum(jnp.where(lane[0:1, :] == grp, counts, 0.0))
        rows = pl.ds(grp * cap, cap)

        @pl.when(count <= cap)
        def _(grp=grp, rows=rows):
            ys_scr[rows, :] = _experts(xs_scr[rows, :], gs_scr[rows, :],
                                       wg_ref, wu_ref, wd_ref, grp).astype(bf16)

        @pl.when(count > cap)
        def _(grp=grp, rows=rows):
            ys_scr[rows, :] = jnp.zeros((cap, ys_scr.shape[1]), bf16)
            out_ref[...] += _experts(h1, gates, wg_ref, wu_ref, wd_ref, grp)

    out_ref[...] += jnp.dot(sel, ys_scr[...], preferred_element_type=f32)

    if with_final:
        o = out_ref[...]
        out_ref[...] = o * _rms_inv(o) * fg_ref[...]


def _moe(h, ffn_gain, wg, bg, we, be, w_gate, w_up, w_down, *, attn=None, w_o=None,
         final_gain=None, tile=512, cap=160):
    N, D = h.shape
    G, E = N_GROUPS, EXPERTS_PER_GROUP
    F = w_gate.shape[-1]
    assert N % tile == 0 and G + G * E <= LANES and cap % 16 == 0 and cap <= tile
    with_attn = attn is not None
    with_final = final_gain is not None

    wr = jnp.zeros((D, LANES), f32).at[:, :G].set(wg).at[:, G:G + G * E].set(we)
    wr_hi = wr.astype(bf16)
    wr_lo = (wr - wr_hi.astype(f32)).astype(bf16)
    br = jnp.zeros((1, LANES), f32).at[0, :G].set(bg).at[0, G:G + G * E].set(be)
    ltri = jnp.asarray(np.tril(np.ones((tile, tile), np.float32), -1), bf16)

    tok = lambda i: (i, 0)
    const2 = lambda i: (0, 0)
    const3 = lambda i: (0, 0, 0)

    def resident(shape):
        return pl.BlockSpec(shape, const2 if len(shape) == 2 else const3,
                            pipeline_mode=pl.Buffered(1))

    in_specs = [pl.BlockSpec((tile, D), tok)]
    args = [h]
    if with_attn:
        in_specs += [pl.BlockSpec((tile, D), tok), resident((D, D))]
        args += [attn, w_o.astype(bf16)]
    in_specs += [
        resident((1, D)), resident((D, LANES)), resident((D, LANES)), resident((1, LANES)),
        resident((tile, tile)),
        resident((G * E, D, F)), resident((G * E, D, F)), resident((G * E, F, D)),
    ]
    args += [ffn_gain.reshape(1, D), wr_hi, wr_lo, br, ltri,
             w_gate.astype(bf16), w_up.astype(bf16), w_down.astype(bf16)]
    if with_final:
        in_specs.append(resident((1, D)))
        args.append(final_gain.reshape(1, D))

    return pl.pallas_call(
        functools.partial(_moe_kernel, with_attn=with_attn, with_final=with_final, cap=cap),
        grid=(N // tile,),
        in_specs=in_specs,
        out_specs=pl.BlockSpec((tile, D), tok),
        out_shape=jax.ShapeDtypeStruct((N, D), f32),
        scratch_shapes=[pltpu.VMEM((G * cap, D), bf16), pltpu.VMEM((G * cap, LANES), f32),
                        pltpu.VMEM((G * cap, D), bf16)],
        compiler_params=pltpu.CompilerParams(
            dimension_semantics=("parallel",), vmem_limit_bytes=VMEM_LIMIT),
        name="moe_final" if with_final else "moe",
    )(*args)


def _kvq_kernel(h_ref, gkv_ref, gq_ref, wk_ref, wv_ref, wq_ref, wf_ref, bf_ref, tri_ref,
                selq_ref, selk_ref, q_ref, qa_ref, k_ref, ka_ref, vt_ref, carry_scr, *, q_scale):
    t = pl.program_id(1)

    @pl.when(t == 0)
    def _():
        carry_scr[...] = jnp.zeros_like(carry_scr)

    x = h_ref[0]
    xn = x * _rms_inv(x)
    hk = (xn * gkv_ref[...]).astype(bf16)
    hq = (xn * gq_ref[...]).astype(bf16)
    k_ref[0] = jnp.dot(hk, wk_ref[...], preferred_element_type=f32).astype(bf16)
    vt_ref[0] = lax.dot_general(wv_ref[...], hk, (((1,), (1,)), ((), ())),
                                preferred_element_type=f32).astype(bf16)
    q_ref[0] = (jnp.dot(hq, wq_ref[...], preferred_element_type=f32) * q_scale).astype(bf16)

    z = jnp.dot(hk, wf_ref[...], preferred_element_type=f32) + bf_ref[...]
    lane = lax.broadcasted_iota(jnp.int32, z.shape, 1)
    log_f = jnp.minimum(z, 0.0) - jnp.log(1.0 + jnp.exp(-jnp.abs(z)))
    lf2 = jnp.where(lane < ONE_LANE, log_f * LOG2E, 0.0)
    hi, mid, lo = _split3(lf2)
    tri = tri_ref[...]
    F = (jnp.dot(tri, hi, preferred_element_type=f32)
         + jnp.dot(tri, mid, preferred_element_type=f32)
         + jnp.dot(tri, lo, preferred_element_type=f32)
         + carry_scr[0:1, :])
    carry_scr[0:1, :] = F[-1:, :]
    fh, fm, fl = _split3(F)
    packed = jnp.where(lane < 16, fh, jnp.where(lane < 32, fm, jnp.where(lane < ONE_LANE, fl,
                       jnp.where(lane == ONE_LANE, 1.0, 0.0).astype(bf16))))
    qa_ref[0] = jnp.dot(packed, selq_ref[...], preferred_element_type=f32).astype(bf16)
    ka_ref[0] = jnp.dot(packed, selk_ref[...], preferred_element_type=f32).astype(bf16)


def _aug_selectors(n_heads):
    n_pairs = n_heads // 2
    selq = np.zeros((LANES, n_pairs * LANES), np.float32)
    selk = np.zeros((LANES, n_pairs * LANES), np.float32)
    for h in range(n_heads):
        base = (h // 2) * LANES + AUG_PER_HEAD * (h % 2)
        for p in range(F_PIECES):
            selq[h + 16 * p, base + p] = 1.0
            selq[ONE_LANE, base + F_PIECES + p] = 1.0
            selk[ONE_LANE, base + p] = 1.0
            selk[h + 16 * p, base + F_PIECES + p] = -1.0
    return jnp.asarray(selq, bf16), jnp.asarray(selk, bf16)


def _kvq(h, kv_gain, q_gain, w_k, w_v, w_q, w_f, b_f, *, tile=512):
    B, S, D = h.shape
    H = w_f.shape[1]
    assert H == 16 and D == H * HEAD_DIM and S % tile == 0
    wf = jnp.zeros((D, LANES), f32)
    bfp = jnp.zeros((1, LANES), f32)
    for p in range(F_PIECES):
        wf = wf.at[:, 16 * p:16 * p + H].set(w_f)
        bfp = bfp.at[0, 16 * p:16 * p + H].set(b_f)
    tri = jnp.asarray(np.tril(np.ones((tile, tile), np.float32)), bf16)
    selq, selk = _aug_selectors(H)
    q_scale = HEAD_DIM ** -0.5 * LOG2E

    const = lambda b, t: (0, 0)
    blk = pl.BlockSpec((1, tile, D), lambda b, t: (b, t, 0))
    out = jax.ShapeDtypeStruct((B, S, D), bf16)
    return pl.pallas_call(
        functools.partial(_kvq_kernel, q_scale=q_scale),
        grid=(B, S // tile),
        in_specs=[
            blk,
            pl.BlockSpec((1, D), const), pl.BlockSpec((1, D), const),
            pl.BlockSpec((D, D), const), pl.BlockSpec((D, D), const), pl.BlockSpec((D, D), const),
            pl.BlockSpec((D, LANES), const), pl.BlockSpec((1, LANES), const),
            pl.BlockSpec((tile, tile), const),
            pl.BlockSpec((LANES, D // 2 // HEAD_DIM * LANES), const),
            pl.BlockSpec((LANES, D // 2 // HEAD_DIM * LANES), const),
        ],
        out_specs=[blk] * 4 + [pl.BlockSpec((1, D, tile), lambda b, t: (b, 0, t))],
        out_shape=[out] * 4 + [jax.ShapeDtypeStruct((B, D, S), bf16)],
        scratch_shapes=[pltpu.VMEM((8, LANES), f32)],
        compiler_params=pltpu.CompilerParams(
            dimension_semantics=("parallel", "arbitrary"), vmem_limit_bytes=VMEM_LIMIT),
        name="kvq",
    )(h, kv_gain.reshape(1, D), q_gain.reshape(1, D), w_k.astype(bf16), w_v.T.astype(bf16),
      w_q.astype(bf16), wf.astype(bf16), bfp, tri, selq, selk)


V_ROWS = HEAD_DIM + 16
Q_CHUNK = 256


def _attn_kernel(q_ref, qa_ref, k_ref, ka_ref, vt_ref, o_ref, kf_scr, vt_scr, acc_scr, m_scr,
                 st0_scr, mx0_scr, st1_scr, mx1_scr, *, tile):
    i = pl.program_id(2)
    T = tile
    n_chunks = T // Q_CHUNK
    S = kf_scr.shape[0]

    @pl.when(i == 0)
    def _():
        kf_scr[:, :LANES] = k_ref[0]
        kf_scr[:, LANES:] = ka_ref[0]
        r = lax.broadcasted_iota(jnp.int32, (V_ROWS - HEAD_DIM, S), 0)
        ones_rows = jnp.where(r == 0, 1.0, 0.0).astype(bf16)
        for hh in range(2):
            vt_scr[hh, :HEAD_DIM, :] = vt_ref[0, hh * HEAD_DIM:(hh + 1) * HEAD_DIM, :]
            vt_scr[hh, HEAD_DIM:, :] = ones_rows

    lane = lax.broadcasted_iota(jnp.int32, (T, LANES), 1)
    zero = jnp.zeros((T, LANES), bf16)
    key = lax.broadcasted_iota(jnp.int32, (T, Q_CHUNK), 0)
    qry = lax.broadcasted_iota(jnp.int32, (T, Q_CHUNK), 1)
    streams = []
    for hh in range(2):
        qm = jnp.where((lane >= HEAD_DIM * hh) & (lane < HEAD_DIM * (hh + 1)), q_ref[0], zero)
        qam = jnp.where((lane >= AUG_PER_HEAD * hh) & (lane < AUG_PER_HEAD * (hh + 1)), qa_ref[0], zero)
        qf = jnp.concatenate([qm, qam], axis=1)
        for c in range(n_chunks):
            streams.append((hh, c, qf[c * Q_CHUNK:(c + 1) * Q_CHUNK, :]))
    m_scr[...] = jnp.full(m_scr.shape, NEG_BIG, f32)
    acc_scr[...] = jnp.zeros(acc_scr.shape, f32)

    def scores(j, st_buf, mx_buf):
        kb = kf_scr[pl.ds(pl.multiple_of(j * T, T), T), :]
        for n, (_, _, qc) in enumerate(streams):
            st = lax.dot_general(kb, qc, (((1,), (1,)), ((), ())), preferred_element_type=f32)
            st_buf[n] = st
            mx_buf[n] = jnp.max(st, axis=0, keepdims=True)

    def softmax_pv(j, st_buf, mx_buf, masked):
        off = pl.multiple_of(j * T, T)
        pts, alphas = [], []
        for n, (hh, c, _) in enumerate(streams):
            st = st_buf[n]
            if masked:
                st = jnp.where(key <= qry + c * Q_CHUNK, st, NEG_BIG)
                mx = jnp.max(st, axis=0, keepdims=True)
            else:
                mx = mx_buf[n]
            m_prev = m_scr[n]
            m_new = jnp.maximum(m_prev, mx)
            alphas.append(jnp.exp2(m_prev - m_new))
            pts.append(jnp.exp2(st - m_new).astype(bf16))
            m_scr[n] = m_new
        for n, (hh, c, _) in enumerate(streams):
            pv = jnp.dot(vt_scr[hh, :, pl.ds(off, T)], pts[n], preferred_element_type=f32)
            acc_scr[n] = acc_scr[n] * alphas[n] + pv

    buf0 = (st0_scr, mx0_scr)
    buf1 = (st1_scr, mx1_scr)

    def step(j, cur, nxt, masked):
        if nxt is not None:
            scores(j + 1, *nxt)
        softmax_pv(j, *cur, masked)

    scores(0, *buf0)

    def body(jj, carry):
        step(2 * jj, buf0, buf1, False)
        step(2 * jj + 1, buf1, buf0, False)
        return carry

    lax.fori_loop(0, i // 2, body, 0)

    @pl.when(i % 2 == 0)
    def _():
        step(i, buf0, None, True)

    @pl.when(i % 2 == 1)
    def _():
        step(i - 1, buf0, buf1, False)
        step(i, buf1, None, True)

    for c in range(n_chunks):
        ot = []
        for hh in range(2):
            acc = acc_scr[hh * n_chunks + c]
            ot.append(acc[:HEAD_DIM] / acc[HEAD_DIM:HEAD_DIM + 1])
        o_ref[0, c * Q_CHUNK:(c + 1) * Q_CHUNK, :] = jnp.concatenate(ot, axis=0).T.astype(bf16)


def _attn(q, qa, k, ka, vt, *, tile=512):
    B, S, D = q.shape
    n_pairs = D // LANES
    assert S % tile == 0 and tile % Q_CHUNK == 0 and LANES == 2 * HEAD_DIM
    n_streams = 2 * (tile // Q_CHUNK)
    qblk = pl.BlockSpec((1, tile, LANES), lambda b, p, i: (b, i, p))
    kblk = pl.BlockSpec((1, S, LANES), lambda b, p, i: (b, 0, p))
    return pl.pallas_call(
        functools.partial(_attn_kernel, tile=tile),
        grid=(B, n_pairs, S // tile),
        in_specs=[qblk, qblk, kblk, kblk, pl.BlockSpec((1, LANES, S), lambda b, p, i: (b, p, 0))],
        out_specs=qblk,
        out_shape=jax.ShapeDtypeStruct((B, S, D), bf16),
        scratch_shapes=[
            pltpu.VMEM((S, 2 * LANES), bf16),
            pltpu.VMEM((2, V_ROWS, S), bf16),
            pltpu.VMEM((n_streams, V_ROWS, Q_CHUNK), f32),
            pltpu.VMEM((n_streams, 1, Q_CHUNK), f32),
        ] + 2 * [pltpu.VMEM((n_streams, tile, Q_CHUNK), f32), pltpu.VMEM((n_streams, 1, Q_CHUNK), f32)],
        compiler_params=pltpu.CompilerParams(
            dimension_semantics=("parallel", "parallel", "arbitrary"), vmem_limit_bytes=VMEM_LIMIT),
        name="attn",
    )(q, qa, k, ka, vt)


def kernel(x, mix_norm, ffn_norm, pool_w, pool_scale, kv_norm, w_k, w_v, w_f, b_f, w_q, w_o,
           router_g, router_g_b, router_e, router_e_b, w_gate, w_up, w_down, final_norm):
    B, S, D = x.shape
    assert mix_norm.shape[0] == 2 and pool_w.shape[0] == 1 and w_q.shape[0] == 1

    def moe(h, layer, **kw):
        return _moe(h, ffn_norm[layer], router_g[layer], router_g_b[layer], router_e[layer],
                    router_e_b[layer], w_gate[layer], w_up[layer], w_down[layer], **kw)

    h = _mix0(x, mix_norm[0], pool_w[0], pool_scale[0])
    h = moe(h.reshape(B * S, D), 0)
    q, qa, k, ka, v = _kvq(h.reshape(B, S, D), kv_norm, mix_norm[1], w_k, w_v, w_q[0], w_f, b_f)
    o = _attn(q, qa, k, ka, v)
    out = moe(h, 1, attn=o.reshape(B * S, D), w_o=w_o[0], final_gain=final_norm)
    return out.reshape(B, S, D)
```

```python
import functools
import math

import numpy as np
import jax
import jax.numpy as jnp
from jax import lax
from jax.experimental import pallas as pl
from jax.experimental.pallas import tpu as pltpu

POOL_WINDOWS = (2, 4, 8, 16)
HEAD_DIM = 64
N_GROUPS = 4
EXPERTS_PER_GROUP = 4
EPS = 1e-6
NEG_BIG = -1e30
LOG2E = math.log2(math.e)

LANES = 128
POOL_HALO = 16
VMEM_LIMIT = 56 * 1024 * 1024

F_PIECES = 3
AUG_PER_HEAD = 2 * F_PIECES
ONE_LANE = 3 * 16

f32 = jnp.float32
bf16 = jnp.bfloat16


def _rms_inv(x):
    return lax.rsqrt(jnp.mean(x * x, axis=-1, keepdims=True) + EPS)


def _split3(x):
    hi = x.astype(bf16)
    r1 = x - hi.astype(f32)
    mid = r1.astype(bf16)
    lo = (r1 - mid.astype(f32)).astype(bf16)
    return hi, mid, lo


def _mix0_kernel(x_ref, halo_ref, g_ref, w_ref, sc_ref, o_ref, *, tile, group_width):
    t = pl.program_id(1)
    x = x_ref[0]
    g = g_ref[...]
    xn = x * _rms_inv(x) * g
    hal = halo_ref[0]
    hn = hal * _rms_inv(hal) * g * (t > 0).astype(f32)
    ext = jnp.concatenate([hn, xn], axis=0)
    pos1 = t * tile + lax.broadcasted_iota(jnp.int32, (tile, 1), 0) + 1
    outs = []
    for gi, w in enumerate(POOL_WINDOWS):
        e = ext[:, gi * group_width:(gi + 1) * group_width]
        s = e
        sh = 1
        while sh < w:
            s = s + pltpu.roll(s, sh, axis=0)
            sh *= 2
        cnt = jnp.minimum(pos1, w).astype(f32)
        diff = (s[POOL_HALO:] / cnt - e[POOL_HALO:]).astype(bf16)
        outs.append(jnp.dot(diff, w_ref[gi], preferred_element_type=f32))
    y = jnp.concatenate(outs, axis=1) * sc_ref[...]
    o_ref[0] = x + y


def _mix0(x, gain, pool_w, pool_scale, *, tile=512):
    B, S, D = x.shape
    G = len(POOL_WINDOWS)
    gw = D // G
    assert all(w & (w - 1) == 0 and w <= POOL_HALO for w in POOL_WINDOWS)
    assert S % tile == 0 and tile % POOL_HALO == 0
    hb = tile // POOL_HALO
    return pl.pallas_call(
        functools.partial(_mix0_kernel, tile=tile, group_width=gw),
        grid=(B, S // tile),
        in_specs=[
            pl.BlockSpec((1, tile, D), lambda b, t: (b, t, 0)),
            pl.BlockSpec((1, POOL_HALO, D), lambda b, t: (b, jnp.maximum(t * hb - 1, 0), 0)),
            pl.BlockSpec((1, D), lambda b, t: (0, 0)),
            pl.BlockSpec((G, gw, gw), lambda b, t: (0, 0, 0)),
            pl.BlockSpec((1, D), lambda b, t: (0, 0)),
        ],
        out_specs=pl.BlockSpec((1, tile, D), lambda b, t: (b, t, 0)),
        out_shape=jax.ShapeDtypeStruct((B, S, D), f32),
        compiler_params=pltpu.CompilerParams(
            dimension_semantics=("parallel", "parallel"), vmem_limit_bytes=VMEM_LIMIT),
        name="mix0",
    )(x, x, gain.reshape(1, D), pool_w.astype(bf16), pool_scale.reshape(1, D))


def _route(logits):
    G, E = N_GROUPS, EXPERTS_PER_GROUP
    lane = lax.broadcasted_iota(jnp.int32, logits.shape, 1).astype(f32)
    far = float(LANES)
    is_g = lane < G
    gl = jnp.where(is_g, logits, -jnp.inf)
    gmax = jnp.max(gl, axis=-1, keepdims=True)
    gidx = jnp.min(jnp.where(gl == gmax, lane, far), axis=-1, keepdims=True)
    gsum = jnp.sum(jnp.where(is_g, jnp.exp(logits - gmax), 0.0), axis=-1, keepdims=True)
    g_w = 1.0 / gsum
    lo = G + E * gidx
    el = jnp.where((lane >= lo) & (lane < lo + E), logits, -jnp.inf)
    m1 = jnp.max(el, axis=-1, keepdims=True)
    i1 = jnp.min(jnp.where(el == m1, lane, far), axis=-1, keepdims=True)
    el2 = jnp.where(lane == i1, -jnp.inf, el)
    m2 = jnp.max(el2, axis=-1, keepdims=True)
    i2 = jnp.min(jnp.where(el2 == m2, lane, far), axis=-1, keepdims=True)
    e2 = jnp.exp(m2 - m1)
    w1 = 1.0 / (1.0 + e2)
    w2 = e2 * w1
    gates = jnp.where(lane == i1, w1 * g_w, jnp.where(lane == i2, w2 * g_w, 0.0))
    return gates, gidx


def _experts(xb, gate_cols, wg_ref, wu_ref, wd_ref, grp):
    G, E = N_GROUPS, EXPERTS_PER_GROUP
    y = None
    for e in range(E):
        eg = E * grp + e
        hg = jnp.dot(xb, wg_ref[eg], preferred_element_type=f32)
        hu = jnp.dot(xb, wu_ref[eg], preferred_element_type=f32)
        act = (hg * (1.0 / (1.0 + jnp.exp(-hg))) * hu * gate_cols[:, G + eg:G + eg + 1]).astype(bf16)
        ye = jnp.dot(act, wd_ref[eg], preferred_element_type=f32)
        y = ye if y is None else y + ye
    return y


def _moe_kernel(*refs, with_attn, with_final, cap):
    refs = list(refs)
    h_ref = refs.pop(0)
    if with_attn:
        a_ref = refs.pop(0)
        wo_ref = refs.pop(0)
    g_ref, wr_ref, br_ref, ltri_ref, wg_ref, wu_ref, wd_ref = refs[:7]
    refs = refs[7:]
    if with_final:
        fg_ref = refs.pop(0)
    (out_ref,) = refs
    T = h_ref.shape[0]
    G = N_GROUPS

    x = h_ref[...]
    if with_attn:
        x = x + jnp.dot(a_ref[...], wo_ref[...], preferred_element_type=f32)
    out_ref[...] = x
    hn = x * _rms_inv(x) * g_ref[...]
    h1 = hn.astype(bf16)
    h2 = (hn - h1.astype(f32)).astype(bf16)
    l1 = jnp.dot(h1, wr_ref[...], preferred_element_type=f32)
    logits = (l1[:, :LANES] + l1[:, LANES:]
              + jnp.dot(h2, wr_ref[:, :LANES], preferred_element_type=f32) + br_ref[...])
    gates, gidx = _route(logits)

    lane = lax.broadcasted_iota(jnp.int32, (T, LANES), 1).astype(f32)
    onehot = jnp.where(lane == gidx, 1.0, 0.0)
    before = jnp.dot(ltri_ref[...], onehot.astype(bf16), preferred_element_type=f32)
    rank = jnp.sum(before * onehot, axis=-1, keepdims=True)
    counts = before[T - 1:T, :] + onehot[T - 1:T, :]

    placed = rank < cap
    slot = jnp.where(placed, gidx * cap + rank, -1.0)
    sel = jnp.where(slot == lax.broadcasted_iota(jnp.int32, (T, G * cap), 1).astype(f32),
                    1.0, 0.0).astype(bf16)
    g_hi = gates.astype(bf16)
    g_lo = (gates - g_hi.astype(f32)).astype(bf16)
    tn = (((0,), (0,)), ((), ()))
    xs = lax.dot_general(sel, h1, tn, preferred_element_type=f32).astype(bf16)
    gs2 = lax.dot_general(sel, jnp.concatenate([g_hi, g_lo], axis=1), tn, preferred_element_type=f32)
    gs = gs2[:, :LANES] + gs2[:, LANES:]
    ys = [_experts(xs[grp * cap:(grp + 1) * cap], gs[grp * cap:(grp + 1) * cap],
                   wg_ref, wu_ref, wd_ref, grp).astype(bf16) for grp in range(G)]
    out_ref[...] += jnp.dot(sel, jnp.concatenate(ys, axis=0), preferred_element_type=f32)

    left_gates = jnp.where(placed, 0.0, gates)
    for grp in range(G):
        count = jnp.sum(jnp.where(lane[0:1, :] == grp, counts, 0.0))

        @pl.when(count > cap)
        def _(grp=grp):
            out_ref[...] += _experts(h1, left_gates, wg_ref, wu_ref, wd_ref, grp)

    if with_final:
        o = out_ref[...]
        out_ref[...] = o * _rms_inv(o) * fg_ref[...]


def _moe(h, ffn_gain, wg, bg, we, be, w_gate, w_up, w_down, *, attn=None, w_o=None,
         final_gain=None, tile=512, cap=160):
    N, D = h.shape
    G, E = N_GROUPS, EXPERTS_PER_GROUP
    F = w_gate.shape[-1]
    assert N % tile == 0 and G + G * E <= LANES and cap % 16 == 0 and cap <= tile
    with_attn = attn is not None
    with_final = final_gain is not None

    wr = jnp.zeros((D, LANES), f32).at[:, :G].set(wg).at[:, G:G + G * E].set(we)
    wr_hi = wr.astype(bf16)
    wr_lo = (wr - wr_hi.astype(f32)).astype(bf16)
    br = jnp.zeros((1, LANES), f32).at[0, :G].set(bg).at[0, G:G + G * E].set(be)
    ltri = jnp.asarray(np.tril(np.ones((tile, tile), np.float32), -1), bf16)

    tok = lambda i: (i, 0)
    const2 = lambda i: (0, 0)
    const3 = lambda i: (0, 0, 0)

    def resident(shape):
        return pl.BlockSpec(shape, const2 if len(shape) == 2 else const3,
                            pipeline_mode=pl.Buffered(1))

    in_specs = [pl.BlockSpec((tile, D), tok)]
    args = [h]
    if with_attn:
        in_specs += [pl.BlockSpec((tile, D), tok), resident((D, D))]
        args += [attn, w_o.astype(bf16)]
    in_specs += [
        resident((1, D)), resident((D, 2 * LANES)), resident((1, LANES)),
        resident((tile, tile)),
        resident((G * E, D, F)), resident((G * E, D, F)), resident((G * E, F, D)),
    ]
    args += [ffn_gain.reshape(1, D), jnp.concatenate([wr_hi, wr_lo], axis=1), br, ltri,
             w_gate.astype(bf16), w_up.astype(bf16), w_down.astype(bf16)]
    if with_final:
        in_specs.append(resident((1, D)))
        args.append(final_gain.reshape(1, D))

    return pl.pallas_call(
        functools.partial(_moe_kernel, with_attn=with_attn, with_final=with_final, cap=cap),
        grid=(N // tile,),
        in_specs=in_specs,
        out_specs=pl.BlockSpec((tile, D), tok),
        out_shape=jax.ShapeDtypeStruct((N, D), f32),
        compiler_params=pltpu.CompilerParams(
            dimension_semantics=("parallel",), vmem_limit_bytes=VMEM_LIMIT),
        name="moe_final" if with_final else "moe",
    )(*args)


def _kvq_kernel(h_ref, gkv_ref, gq_ref, wk_ref, wv_ref, wq_ref, wf_ref, bf_ref, tri_ref,
                selq_ref, selk_ref, qt_ref, qat_ref, k_ref, ka_ref, vt_ref, carry_scr, *, q_scale):
    t = pl.program_id(1)

    @pl.when(t == 0)
    def _():
        carry_scr[...] = jnp.zeros_like(carry_scr)

    x = h_ref[0]
    xn = x * _rms_inv(x)
    hk = (xn * gkv_ref[...]).astype(bf16)
    hq = (xn * gq_ref[...]).astype(bf16)
    k_ref[0] = jnp.dot(hk, wk_ref[...], preferred_element_type=f32).astype(bf16)
    nt = (((1,), (1,)), ((), ()))
    vt_ref[0] = lax.dot_general(wv_ref[...], hk, nt, preferred_element_type=f32).astype(bf16)
    qt_ref[0] = (lax.dot_general(wq_ref[...], hq, nt, preferred_element_type=f32) * q_scale).astype(bf16)

    z = jnp.dot(hk, wf_ref[...], preferred_element_type=f32) + bf_ref[...]
    lane = lax.broadcasted_iota(jnp.int32, z.shape, 1)
    log_f = jnp.minimum(z, 0.0) - jnp.log(1.0 + jnp.exp(-jnp.abs(z)))
    lf2 = jnp.where(lane < ONE_LANE, log_f * LOG2E, 0.0)
    hi, mid, lo = _split3(lf2)
    tri = tri_ref[...]
    F = (jnp.dot(tri, hi, preferred_element_type=f32)
         + jnp.dot(tri, mid, preferred_element_type=f32)
         + jnp.dot(tri, lo, preferred_element_type=f32)
         + carry_scr[0:1, :])
    carry_scr[0:1, :] = F[-1:, :]
    fh, fm, fl = _split3(F)
    packed = jnp.where(lane < 16, fh, jnp.where(lane < 32, fm, jnp.where(lane < ONE_LANE, fl,
                       jnp.where(lane == ONE_LANE, 1.0, 0.0).astype(bf16))))
    qat_ref[0] = lax.dot_general(selq_ref[...], packed, nt, preferred_element_type=f32).astype(bf16)
    ka_ref[0] = jnp.dot(packed, selk_ref[...], preferred_element_type=f32).astype(bf16)


def _aug_selectors(n_heads):
    n_pairs = n_heads // 2
    selq = np.zeros((LANES, n_pairs * LANES), np.float32)
    selk = np.zeros((LANES, n_pairs * LANES), np.float32)
    for h in range(n_heads):
        base = (h // 2) * LANES + AUG_PER_HEAD * (h % 2)
        for p in range(F_PIECES):
            selq[h + 16 * p, base + p] = 1.0
            selq[ONE_LANE, base + F_PIECES + p] = 1.0
            selk[ONE_LANE, base + p] = 1.0
            selk[h + 16 * p, base + F_PIECES + p] = -1.0
    return jnp.asarray(selq.T, bf16), jnp.asarray(selk, bf16)


def _kvq(h, kv_gain, q_gain, w_k, w_v, w_q, w_f, b_f, *, tile=512):
    B, S, D = h.shape
    H = w_f.shape[1]
    assert H == 16 and D == H * HEAD_DIM and S % tile == 0
    wf = jnp.zeros((D, LANES), f32)
    bfp = jnp.zeros((1, LANES), f32)
    for p in range(F_PIECES):
        wf = wf.at[:, 16 * p:16 * p + H].set(w_f)
        bfp = bfp.at[0, 16 * p:16 * p + H].set(b_f)
    tri = jnp.asarray(np.tril(np.ones((tile, tile), np.float32)), bf16)
    selq, selk = _aug_selectors(H)
    q_scale = HEAD_DIM ** -0.5 * LOG2E

    const = lambda b, t: (0, 0)
    blk = pl.BlockSpec((1, tile, D), lambda b, t: (b, t, 0))
    out = jax.ShapeDtypeStruct((B, S, D), bf16)
    blk_t = pl.BlockSpec((1, D, tile), lambda b, t: (b, 0, t))
    out_t = jax.ShapeDtypeStruct((B, D, S), bf16)
    return pl.pallas_call(
        functools.partial(_kvq_kernel, q_scale=q_scale),
        grid=(B, S // tile),
        in_specs=[
            blk,
            pl.BlockSpec((1, D), const), pl.BlockSpec((1, D), const),
            pl.BlockSpec((D, D), const), pl.BlockSpec((D, D), const), pl.BlockSpec((D, D), const),
            pl.BlockSpec((D, LANES), const), pl.BlockSpec((1, LANES), const),
            pl.BlockSpec((tile, tile), const),
            pl.BlockSpec((D // 2 // HEAD_DIM * LANES, LANES), const),
            pl.BlockSpec((LANES, D // 2 // HEAD_DIM * LANES), const),
        ],
        out_specs=[blk_t, blk_t, blk, blk, blk_t],
        out_shape=[out_t, out_t, out, out, out_t],
        scratch_shapes=[pltpu.VMEM((8, LANES), f32)],
        compiler_params=pltpu.CompilerParams(
            dimension_semantics=("parallel", "arbitrary"), vmem_limit_bytes=VMEM_LIMIT),
        name="kvq",
    )(h, kv_gain.reshape(1, D), q_gain.reshape(1, D), w_k.astype(bf16), w_v.T.astype(bf16),
      w_q.T.astype(bf16), wf.astype(bf16), bfp, tri, selq, selk)


V_ROWS = HEAD_DIM + 16
Q_CHUNK = 256


def _attn_kernel(qt_ref, qat_ref, k_ref, ka_ref, vt_ref, o_ref, kf_scr, vt_scr, acc_scr, m_scr,
                 st0_scr, mx0_scr, st1_scr, mx1_scr, *, tq, tk):
    i = pl.program_id(2)
    T = tk
    n_chunks = tq // Q_CHUNK
    half = n_chunks // 2
    S = kf_scr.shape[0]

    @pl.when(i == 0)
    def _():
        kf_scr[:, :LANES] = k_ref[0]
        kf_scr[:, LANES:] = ka_ref[0]
        r = lax.broadcasted_iota(jnp.int32, (V_ROWS - HEAD_DIM, S), 0)
        ones_rows = jnp.where(r == 0, 1.0, 0.0).astype(bf16)
        for hh in range(2):
            vt_scr[hh, :HEAD_DIM, :] = vt_ref[0, hh * HEAD_DIM:(hh + 1) * HEAD_DIM, :]
            vt_scr[hh, HEAD_DIM:, :] = ones_rows

    feat = lax.broadcasted_iota(jnp.int32, (LANES, tq), 0)
    zero = jnp.zeros((LANES, tq), bf16)
    key = lax.broadcasted_iota(jnp.int32, (T, Q_CHUNK), 0)
    qry = lax.broadcasted_iota(jnp.int32, (T, Q_CHUNK), 1)
    streams = []
    for hh in range(2):
        qm = jnp.where((feat >= HEAD_DIM * hh) & (feat < HEAD_DIM * (hh + 1)), qt_ref[0], zero)
        qam = jnp.where((feat >= AUG_PER_HEAD * hh) & (feat < AUG_PER_HEAD * (hh + 1)), qat_ref[0], zero)
        qft = jnp.concatenate([qm, qam], axis=0)
        for c in range(n_chunks):
            streams.append((hh, c, qft[:, c * Q_CHUNK:(c + 1) * Q_CHUNK]))
    m_scr[...] = jnp.full(m_scr.shape, NEG_BIG, f32)
    acc_scr[...] = jnp.zeros(acc_scr.shape, f32)

    every = list(range(len(streams)))
    early = [n for n in every if streams[n][1] < half]
    late = [n for n in every if streams[n][1] >= half]

    def scores(j, st_buf, mx_buf, which):
        kb = kf_scr[pl.ds(pl.multiple_of(j * T, T), T), :]
        for n in which:
            st = jnp.dot(kb, streams[n][2], preferred_element_type=f32)
            st_buf[n] = st
            mx_buf[n] = jnp.max(st, axis=0, keepdims=True)

    def softmax_pv(j, st_buf, mx_buf, plain, masked, key_shift=0):
        off = pl.multiple_of(j * T, T)
        pts, alphas = {}, {}
        for n in sorted(plain + masked):
            hh, c, _ = streams[n]
            st = st_buf[n]
            if n in masked:
                st = jnp.where(key + key_shift <= qry + c * Q_CHUNK, st, NEG_BIG)
                mx = jnp.max(st, axis=0, keepdims=True)
            else:
                mx = mx_buf[n]
            m_prev = m_scr[n]
            m_new = jnp.maximum(m_prev, mx)
            alphas[n] = jnp.exp2(m_prev - m_new)
            pts[n] = jnp.exp2(st - m_new).astype(bf16)
            m_scr[n] = m_new
        for n in sorted(plain + masked):
            hh = streams[n][0]
            pv = jnp.dot(vt_scr[hh, :, pl.ds(off, T)], pts[n], preferred_element_type=f32)
            acc_scr[n] = acc_scr[n] * alphas[n] + pv

    buf0 = (st0_scr, mx0_scr)
    buf1 = (st1_scr, mx1_scr)
    scores(0, *buf0, every)

    def body(jj, carry):
        scores(2 * jj + 1, *buf1, every)
        softmax_pv(2 * jj, *buf0, every, [])
        scores(2 * jj + 2, *buf0, every)
        softmax_pv(2 * jj + 1, *buf1, every, [])
        return carry

    lax.fori_loop(0, i, body, 0)
    scores(2 * i + 1, *buf1, late)
    softmax_pv(2 * i, *buf0, late, early)
    softmax_pv(2 * i + 1, *buf1, [], late, key_shift=T)

    for c in range(n_chunks):
        ot = []
        for hh in range(2):
            acc = acc_scr[hh * n_chunks + c]
            ot.append(acc[:HEAD_DIM] / acc[HEAD_DIM:HEAD_DIM + 1])
        o_ref[0, c * Q_CHUNK:(c + 1) * Q_CHUNK, :] = jnp.concatenate(ot, axis=0).T.astype(bf16)


def _attn(qt, qat, k, ka, vt, *, tk=512):
    B, S, D = k.shape
    n_pairs = D // LANES
    tq = 2 * tk
    assert S % tq == 0 and tk % Q_CHUNK == 0 and LANES == 2 * HEAD_DIM
    n_streams = 2 * (tq // Q_CHUNK)
    qblk = pl.BlockSpec((1, LANES, tq), lambda b, p, i: (b, p, i))
    kblk = pl.BlockSpec((1, S, LANES), lambda b, p, i: (b, 0, p))
    return pl.pallas_call(
        functools.partial(_attn_kernel, tq=tq, tk=tk),
        grid=(B, n_pairs, S // tq),
        in_specs=[qblk, qblk, kblk, kblk, pl.BlockSpec((1, LANES, S), lambda b, p, i: (b, p, 0))],
        out_specs=pl.BlockSpec((1, tq, LANES), lambda b, p, i: (b, i, p)),
        out_shape=jax.ShapeDtypeStruct((B, S, D), bf16),
        scratch_shapes=[
            pltpu.VMEM((S, 2 * LANES), bf16),
            pltpu.VMEM((2, V_ROWS, S), bf16),
            pltpu.VMEM((n_streams, V_ROWS, Q_CHUNK), f32),
            pltpu.VMEM((n_streams, 1, Q_CHUNK), f32),
        ] + 2 * [pltpu.VMEM((n_streams, tk, Q_CHUNK), f32), pltpu.VMEM((n_streams, 1, Q_CHUNK), f32)],
        compiler_params=pltpu.CompilerParams(
            dimension_semantics=("parallel", "parallel", "arbitrary"), vmem_limit_bytes=VMEM_LIMIT),
        name="attn",
    )(qt, qat, k, ka, vt)


def kernel(x, mix_norm, ffn_norm, pool_w, pool_scale, kv_norm, w_k, w_v, w_f, b_f, w_q, w_o,
           router_g, router_g_b, router_e, router_e_b, w_gate, w_up, w_down, final_norm):
    B, S, D = x.shape
    assert mix_norm.shape[0] == 2 and pool_w.shape[0] == 1 and w_q.shape[0] == 1

    def moe(h, layer, **kw):
        return _moe(h, ffn_norm[layer], router_g[layer], router_g_b[layer], router_e[layer],
                    router_e_b[layer], w_gate[layer], w_up[layer], w_down[layer], **kw)

    h = _mix0(x, mix_norm[0], pool_w[0], pool_scale[0])
    h = moe(h.reshape(B * S, D), 0)
    qt, qat, k, ka, vt = _kvq(h.reshape(B, S, D), kv_norm, mix_norm[1], w_k, w_v, w_q[0], w_f, b_f)
    o = _attn(qt, qat, k, ka, vt)
    out = moe(h, 1, attn=o.reshape(B * S, D), w_o=w_o[0], final_gain=final_norm)
    return out.reshape(B, S, D)
```

```python
import functools
import math

import numpy as np
import jax
import jax.numpy as jnp
from jax import lax
from jax.experimental import pallas as pl
from jax.experimental.pallas import tpu as pltpu

POOL_WINDOWS = (2, 4, 8, 16)
HEAD_DIM = 64
N_GROUPS = 4
EXPERTS_PER_GROUP = 4
EPS = 1e-6
NEG_BIG = -1e30
LOG2E = math.log2(math.e)

LANES = 128
POOL_HALO = 16
VMEM_LIMIT = 56 * 1024 * 1024

F_PIECES = 3
AUG_PER_HEAD = 2 * F_PIECES
ONE_LANE = 3 * 16

f32 = jnp.float32
bf16 = jnp.bfloat16


def _rms_inv(x):
    return lax.rsqrt(jnp.mean(x * x, axis=-1, keepdims=True) + EPS)


def _split3(x):
    hi = x.astype(bf16)
    r1 = x - hi.astype(f32)
    mid = r1.astype(bf16)
    lo = (r1 - mid.astype(f32)).astype(bf16)
    return hi, mid, lo


def _mix0_kernel(x_ref, halo_ref, g_ref, w_ref, sc_ref, o_ref, *, tile, group_width):
    t = pl.program_id(1)
    x = x_ref[0]
    g = g_ref[...]
    xn = x * _rms_inv(x) * g
    hal = halo_ref[0]
    hn = hal * _rms_inv(hal) * g * (t > 0).astype(f32)
    ext = jnp.concatenate([hn, xn], axis=0)
    pos1 = t * tile + lax.broadcasted_iota(jnp.int32, (tile, 1), 0) + 1
    outs = []
    for gi, w in enumerate(POOL_WINDOWS):
        e = ext[:, gi * group_width:(gi + 1) * group_width]
        s = e
        sh = 1
        while sh < w:
            s = s + pltpu.roll(s, sh, axis=0)
            sh *= 2
        cnt = jnp.minimum(pos1, w).astype(f32)
        diff = (s[POOL_HALO:] / cnt - e[POOL_HALO:]).astype(bf16)
        outs.append(jnp.dot(diff, w_ref[gi], preferred_element_type=f32))
    y = jnp.concatenate(outs, axis=1) * sc_ref[...]
    o_ref[0] = x + y


def _mix0(x, gain, pool_w, pool_scale, *, tile=512):
    B, S, D = x.shape
    G = len(POOL_WINDOWS)
    gw = D // G
    assert all(w & (w - 1) == 0 and w <= POOL_HALO for w in POOL_WINDOWS)
    assert S % tile == 0 and tile % POOL_HALO == 0
    hb = tile // POOL_HALO
    return pl.pallas_call(
        functools.partial(_mix0_kernel, tile=tile, group_width=gw),
        grid=(B, S // tile),
        in_specs=[
            pl.BlockSpec((1, tile, D), lambda b, t: (b, t, 0)),
            pl.BlockSpec((1, POOL_HALO, D), lambda b, t: (b, jnp.maximum(t * hb - 1, 0), 0)),
            pl.BlockSpec((1, D), lambda b, t: (0, 0)),
            pl.BlockSpec((G, gw, gw), lambda b, t: (0, 0, 0)),
            pl.BlockSpec((1, D), lambda b, t: (0, 0)),
        ],
        out_specs=pl.BlockSpec((1, tile, D), lambda b, t: (b, t, 0)),
        out_shape=jax.ShapeDtypeStruct((B, S, D), f32),
        compiler_params=pltpu.CompilerParams(
            dimension_semantics=("parallel", "parallel"), vmem_limit_bytes=VMEM_LIMIT),
        name="mix0",
    )(x, x, gain.reshape(1, D), pool_w.astype(bf16), pool_scale.reshape(1, D))


def _route(logits):
    G, E = N_GROUPS, EXPERTS_PER_GROUP
    lane = lax.broadcasted_iota(jnp.int32, logits.shape, 1).astype(f32)
    far = float(LANES)
    is_g = lane < G
    gl = jnp.where(is_g, logits, -jnp.inf)
    gmax = jnp.max(gl, axis=-1, keepdims=True)
    gidx = jnp.min(jnp.where(gl == gmax, lane, far), axis=-1, keepdims=True)
    gsum = jnp.sum(jnp.where(is_g, jnp.exp(logits - gmax), 0.0), axis=-1, keepdims=True)
    g_w = 1.0 / gsum
    lo = G + E * gidx
    el = jnp.where((lane >= lo) & (lane < lo + E), logits, -jnp.inf)
    m1 = jnp.max(el, axis=-1, keepdims=True)
    i1 = jnp.min(jnp.where(el == m1, lane, far), axis=-1, keepdims=True)
    el2 = jnp.where(lane == i1, -jnp.inf, el)
    m2 = jnp.max(el2, axis=-1, keepdims=True)
    i2 = jnp.min(jnp.where(el2 == m2, lane, far), axis=-1, keepdims=True)
    e2 = jnp.exp(m2 - m1)
    w1 = 1.0 / (1.0 + e2)
    w2 = e2 * w1
    gates = jnp.where(lane == i1, w1 * g_w, jnp.where(lane == i2, w2 * g_w, 0.0))
    return gates, gidx


def _experts(xb, gate_cols, wg_ref, wu_ref, wd_ref, grp):
    G, E = N_GROUPS, EXPERTS_PER_GROUP
    y = None
    for e in range(E):
        eg = E * grp + e
        hg = jnp.dot(xb, wg_ref[eg], preferred_element_type=f32)
        hu = jnp.dot(xb, wu_ref[eg], preferred_element_type=f32)
        act = (hg * (1.0 / (1.0 + jnp.exp(-hg))) * hu * gate_cols[:, G + eg:G + eg + 1]).astype(bf16)
        ye = jnp.dot(act, wd_ref[eg], preferred_element_type=f32)
        y = ye if y is None else y + ye
    return y


def _moe_kernel(*refs, with_attn, with_final, cap):
    refs = list(refs)
    h_ref = refs.pop(0)
    if with_attn:
        a_ref = refs.pop(0)
        wo_ref = refs.pop(0)
    g_ref, wr_ref, br_ref, ltri_ref, wg_ref, wu_ref, wd_ref = refs[:7]
    refs = refs[7:]
    if with_final:
        fg_ref = refs.pop(0)
    (out_ref,) = refs
    T = h_ref.shape[0]
    G = N_GROUPS

    x = h_ref[...]
    if with_attn:
        x = x + jnp.dot(a_ref[...], wo_ref[...], preferred_element_type=f32)
    out_ref[...] = x
    hn = x * _rms_inv(x) * g_ref[...]
    h1 = hn.astype(bf16)
    h2 = (hn - h1.astype(f32)).astype(bf16)
    l1 = jnp.dot(h1, wr_ref[...], preferred_element_type=f32)
    logits = (l1[:, :LANES] + l1[:, LANES:]
              + jnp.dot(h2, wr_ref[:, :LANES], preferred_element_type=f32) + br_ref[...])
    gates, gidx = _route(logits)

    lane = lax.broadcasted_iota(jnp.int32, (T, LANES), 1).astype(f32)
    onehot = jnp.where(lane == gidx, 1.0, 0.0)
    before = jnp.dot(ltri_ref[...], onehot.astype(bf16), preferred_element_type=f32)
    rank = jnp.sum(before * onehot, axis=-1, keepdims=True)
    counts = before[T - 1:T, :] + onehot[T - 1:T, :]

    placed = rank < cap
    slot = jnp.where(placed, gidx * cap + rank, -1.0)
    sel = jnp.where(slot == lax.broadcasted_iota(jnp.int32, (T, G * cap), 1).astype(f32),
                    1.0, 0.0).astype(bf16)
    g_hi = gates.astype(bf16)
    g_lo = (gates - g_hi.astype(f32)).astype(bf16)
    tn = (((0,), (0,)), ((), ()))
    xs = lax.dot_general(sel, h1, tn, preferred_element_type=f32).astype(bf16)
    gs2 = lax.dot_general(sel, jnp.concatenate([g_hi, g_lo], axis=1), tn, preferred_element_type=f32)
    gs = gs2[:, :LANES] + gs2[:, LANES:]
    E = EXPERTS_PER_GROUP
    ys = [None] * G
    pending = None
    for k in range(G * E):
        grp = k // E
        xg = xs[grp * cap:(grp + 1) * cap]
        hg = jnp.dot(xg, wg_ref[k], preferred_element_type=f32)
        hu = jnp.dot(xg, wu_ref[k], preferred_element_type=f32)
        if pending is not None:
            pending()
        gate = gs[grp * cap:(grp + 1) * cap, G + k:G + k + 1]
        act = (hg * (1.0 / (1.0 + jnp.exp(-hg))) * hu * gate).astype(bf16)

        def pending(k=k, grp=grp, act=act):
            ye = jnp.dot(act, wd_ref[k], preferred_element_type=f32)
            ys[grp] = ye if ys[grp] is None else ys[grp] + ye
    pending()
    ys = jnp.concatenate([y.astype(bf16) for y in ys], axis=0)
    out_ref[...] += jnp.dot(sel, ys, preferred_element_type=f32)

    left_gates = jnp.where(placed, 0.0, gates)
    for grp in range(G):
        count = jnp.sum(jnp.where(lane[0:1, :] == grp, counts, 0.0))

        @pl.when(count > cap)
        def _(grp=grp):
            out_ref[...] += _experts(h1, left_gates, wg_ref, wu_ref, wd_ref, grp)

    if with_final:
        o = out_ref[...]
        out_ref[...] = o * _rms_inv(o) * fg_ref[...]


def _moe(h, ffn_gain, wg, bg, we, be, w_gate, w_up, w_down, *, attn=None, w_o=None,
         final_gain=None, tile=512, cap=160):
    N, D = h.shape
    G, E = N_GROUPS, EXPERTS_PER_GROUP
    F = w_gate.shape[-1]
    assert N % tile == 0 and G + G * E <= LANES and cap % 16 == 0 and cap <= tile
    with_attn = attn is not None
    with_final = final_gain is not None

    wr = jnp.zeros((D, LANES), f32).at[:, :G].set(wg).at[:, G:G + G * E].set(we)
    wr_hi = wr.astype(bf16)
    wr_lo = (wr - wr_hi.astype(f32)).astype(bf16)
    br = jnp.zeros((1, LANES), f32).at[0, :G].set(bg).at[0, G:G + G * E].set(be)
    ltri = jnp.asarray(np.tril(np.ones((tile, tile), np.float32), -1), bf16)

    tok = lambda i: (i, 0)
    const2 = lambda i: (0, 0)
    const3 = lambda i: (0, 0, 0)

    def resident(shape):
        return pl.BlockSpec(shape, const2 if len(shape) == 2 else const3,
                            pipeline_mode=pl.Buffered(1))

    in_specs = [pl.BlockSpec((tile, D), tok)]
    args = [h]
    if with_attn:
        in_specs += [pl.BlockSpec((tile, D), tok), resident((D, D))]
        args += [attn, w_o.astype(bf16)]
    in_specs += [
        resident((1, D)), resident((D, 2 * LANES)), resident((1, LANES)),
        resident((tile, tile)),
        resident((G * E, D, F)), resident((G * E, D, F)), resident((G * E, F, D)),
    ]
    args += [ffn_gain.reshape(1, D), jnp.concatenate([wr_hi, wr_lo], axis=1), br, ltri,
             w_gate.astype(bf16), w_up.astype(bf16), w_down.astype(bf16)]
    if with_final:
        in_specs.append(resident((1, D)))
        args.append(final_gain.reshape(1, D))

    return pl.pallas_call(
        functools.partial(_moe_kernel, with_attn=with_attn, with_final=with_final, cap=cap),
        grid=(N // tile,),
        in_specs=in_specs,
        out_specs=pl.BlockSpec((tile, D), tok),
        out_shape=jax.ShapeDtypeStruct((N, D), f32),
        compiler_params=pltpu.CompilerParams(
            dimension_semantics=("parallel",), vmem_limit_bytes=VMEM_LIMIT),
        name="moe_final" if with_final else "moe",
    )(*args)


def _kvq_kernel(h_ref, gkv_ref, gq_ref, wk_ref, wv_ref, wq_ref, wf_ref, bf_ref, tri_ref,
                selq_ref, selk_ref, qt_ref, qat_ref, k_ref, ka_ref, vt_ref, carry_scr, *, q_scale):
    t = pl.program_id(1)

    @pl.when(t == 0)
    def _():
        carry_scr[...] = jnp.zeros_like(carry_scr)

    x = h_ref[0]
    xn = x * _rms_inv(x)
    hk = (xn * gkv_ref[...]).astype(bf16)
    hq = (xn * gq_ref[...]).astype(bf16)
    z = jnp.dot(hk, wf_ref[...], preferred_element_type=f32) + bf_ref[...]
    k_ref[0] = jnp.dot(hk, wk_ref[...], preferred_element_type=f32).astype(bf16)
    lane = lax.broadcasted_iota(jnp.int32, z.shape, 1)
    log_f = jnp.minimum(z, 0.0) - jnp.log(1.0 + jnp.exp(-jnp.abs(z)))
    lf2 = jnp.where(lane < ONE_LANE, log_f * LOG2E, 0.0)
    hi, mid, lo = _split3(lf2)
    tri = tri_ref[...]
    F = (jnp.dot(tri, hi, preferred_element_type=f32)
         + jnp.dot(tri, mid, preferred_element_type=f32)
         + jnp.dot(tri, lo, preferred_element_type=f32)
         + carry_scr[0:1, :])
    carry_scr[0:1, :] = F[-1:, :]
    nt = (((1,), (1,)), ((), ()))
    vt_ref[0] = lax.dot_general(wv_ref[...], hk, nt, preferred_element_type=f32).astype(bf16)
    fh, fm, fl = _split3(F)
    packed = jnp.where(lane < 16, fh, jnp.where(lane < 32, fm, jnp.where(lane < ONE_LANE, fl,
                       jnp.where(lane == ONE_LANE, 1.0, 0.0).astype(bf16))))
    qat_ref[0] = lax.dot_general(selq_ref[...], packed, nt, preferred_element_type=f32).astype(bf16)
    ka_ref[0] = jnp.dot(packed, selk_ref[...], preferred_element_type=f32).astype(bf16)
    qt_ref[0] = (lax.dot_general(wq_ref[...], hq, nt, preferred_element_type=f32) * q_scale).astype(bf16)


def _aug_selectors(n_heads):
    n_pairs = n_heads // 2
    selq = np.zeros((LANES, n_pairs * LANES), np.float32)
    selk = np.zeros((LANES, n_pairs * LANES), np.float32)
    for h in range(n_heads):
        base = (h // 2) * LANES + AUG_PER_HEAD * (h % 2)
        for p in range(F_PIECES):
            selq[h + 16 * p, base + p] = 1.0
            selq[ONE_LANE, base + F_PIECES + p] = 1.0
            selk[ONE_LANE, base + p] = 1.0
            selk[h + 16 * p, base + F_PIECES + p] = -1.0
    return jnp.asarray(selq.T, bf16), jnp.asarray(selk, bf16)


def _kvq(h, kv_gain, q_gain, w_k, w_v, w_q, w_f, b_f, *, tile=512):
    B, S, D = h.shape
    H = w_f.shape[1]
    assert H == 16 and D == H * HEAD_DIM and S % tile == 0
    wf = jnp.zeros((D, LANES), f32)
    bfp = jnp.zeros((1, LANES), f32)
    for p in range(F_PIECES):
        wf = wf.at[:, 16 * p:16 * p + H].set(w_f)
        bfp = bfp.at[0, 16 * p:16 * p + H].set(b_f)
    tri = jnp.asarray(np.tril(np.ones((tile, tile), np.float32)), bf16)
    selq, selk = _aug_selectors(H)
    q_scale = HEAD_DIM ** -0.5 * LOG2E

    const = lambda b, t: (0, 0)
    blk = pl.BlockSpec((1, tile, D), lambda b, t: (b, t, 0))
    out = jax.ShapeDtypeStruct((B, S, D), bf16)
    blk_t = pl.BlockSpec((1, D, tile), lambda b, t: (b, 0, t))
    out_t = jax.ShapeDtypeStruct((B, D, S), bf16)
    return pl.pallas_call(
        functools.partial(_kvq_kernel, q_scale=q_scale),
        grid=(B, S // tile),
        in_specs=[
            blk,
            pl.BlockSpec((1, D), const), pl.BlockSpec((1, D), const),
            pl.BlockSpec((D, D), const), pl.BlockSpec((D, D), const), pl.BlockSpec((D, D), const),
            pl.BlockSpec((D, LANES), const), pl.BlockSpec((1, LANES), const),
            pl.BlockSpec((tile, tile), const),
            pl.BlockSpec((D // 2 // HEAD_DIM * LANES, LANES), const),
            pl.BlockSpec((LANES, D // 2 // HEAD_DIM * LANES), const),
        ],
        out_specs=[blk_t, blk_t, blk, blk, blk_t],
        out_shape=[out_t, out_t, out, out, out_t],
        scratch_shapes=[pltpu.VMEM((8, LANES), f32)],
        compiler_params=pltpu.CompilerParams(
            dimension_semantics=("parallel", "arbitrary"), vmem_limit_bytes=VMEM_LIMIT),
        name="kvq",
    )(h, kv_gain.reshape(1, D), q_gain.reshape(1, D), w_k.astype(bf16), w_v.T.astype(bf16),
      w_q.T.astype(bf16), wf.astype(bf16), bfp, tri, selq, selk)


V_ROWS = HEAD_DIM + 16
Q_CHUNK = 256
PV_LAG = 1


def _attn_kernel(qt_ref, qat_ref, k_ref, ka_ref, vt_ref, o_ref, kf_scr, vt_scr, qf_scr, acc_scr, m_scr,
                 st0_scr, mx0_scr, st1_scr, mx1_scr, *, tq, tk):
    T = tk
    n_chunks = tq // Q_CHUNK
    half = n_chunks // 2
    S = kf_scr.shape[0]
    n_tiles = S // tq

    kf_scr[:, :LANES] = k_ref[0]
    kf_scr[:, LANES:] = ka_ref[0]
    r = lax.broadcasted_iota(jnp.int32, (V_ROWS - HEAD_DIM, S), 0)
    ones_rows = jnp.where(r == 0, 1.0, 0.0).astype(bf16)
    for hh in range(2):
        vt_scr[hh, :HEAD_DIM, :] = vt_ref[0, hh * HEAD_DIM:(hh + 1) * HEAD_DIM, :]
        vt_scr[hh, HEAD_DIM:, :] = ones_rows

    feat = lax.broadcasted_iota(jnp.int32, (LANES, tq), 0)
    zero = jnp.zeros((LANES, tq), bf16)
    key = lax.broadcasted_iota(jnp.int32, (T, Q_CHUNK), 0)
    qry = lax.broadcasted_iota(jnp.int32, (T, Q_CHUNK), 1)
    streams = [(hh, c) for hh in range(2) for c in range(n_chunks)]
    every = list(range(len(streams)))
    early = [n for n in every if streams[n][1] < half]
    late = [n for n in every if streams[n][1] >= half]

    def load_queries(i):
        cols = pl.ds(pl.multiple_of(i * tq, tq), tq)
        qt = qt_ref[0, :, cols]
        qat = qat_ref[0, :, cols]
        for hh in range(2):
            qf_scr[hh, :LANES, :] = jnp.where(
                (feat >= HEAD_DIM * hh) & (feat < HEAD_DIM * (hh + 1)), qt, zero)
            qf_scr[hh, LANES:, :] = jnp.where(
                (feat >= AUG_PER_HEAD * hh) & (feat < AUG_PER_HEAD * (hh + 1)), qat, zero)

    def reset_stats():
        m_scr[...] = jnp.full(m_scr.shape, NEG_BIG, f32)
        acc_scr[...] = jnp.zeros(acc_scr.shape, f32)

    def scores(j, st_buf, mx_buf, which):
        kb = kf_scr[pl.ds(pl.multiple_of(j * T, T), T), :]
        for n in which:
            hh, c = streams[n]
            qc = qf_scr[hh, :, c * Q_CHUNK:(c + 1) * Q_CHUNK]
            st = jnp.dot(kb, qc, preferred_element_type=f32)
            st_buf[n] = st
            mx_buf[n] = jnp.max(st, axis=0, keepdims=True)

    def fused_step(j, cur, plain, masked, key_shift=0, nxt_j=None, nxt=None, nxt_streams=()):
        st_c, mx_c = cur
        off = pl.multiple_of(j * T, T)
        todo = sorted(plain + masked)
        issue = list(nxt_streams)
        if issue:
            kb = kf_scr[pl.ds(pl.multiple_of(nxt_j * T, T), T), :]
        pending = []
        for idx in range(max(len(todo), len(issue))):
            if idx < len(todo):
                n = todo[idx]
                hh, c = streams[n]
                st = st_c[n]
                if n in masked:
                    st = jnp.where(key + key_shift <= qry + c * Q_CHUNK, st, NEG_BIG)
                    mx = jnp.max(st, axis=0, keepdims=True)
                else:
                    mx = mx_c[n]
                m_prev = m_scr[n]
                m_new = jnp.maximum(m_prev, mx)
                alpha = jnp.exp2(m_prev - m_new)
                pt = jnp.exp2(st - m_new).astype(bf16)
                m_scr[n] = m_new

                def value_product(n=n, hh=hh, pt=pt, alpha=alpha):
                    pv = jnp.dot(vt_scr[hh, :, pl.ds(off, T)], pt, preferred_element_type=f32)
                    acc_scr[n] = acc_scr[n] * alpha + pv
                pending.append(value_product)
            if idx < len(issue):
                n = issue[idx]
                hh, c = streams[n]
                qc = qf_scr[hh, :, c * Q_CHUNK:(c + 1) * Q_CHUNK]
                st = jnp.dot(kb, qc, preferred_element_type=f32)
                nxt[0][n] = st
                nxt[1][n] = jnp.max(st, axis=0, keepdims=True)
            if len(pending) > PV_LAG + 1:
                pending.pop(0)()
        for value_product in pending:
            value_product()

    buf0 = (st0_scr, mx0_scr)
    buf1 = (st1_scr, mx1_scr)

    def body(jj, carry):
        fused_step(2 * jj, buf0, every, [], nxt_j=2 * jj + 1, nxt=buf1, nxt_streams=every)
        fused_step(2 * jj + 1, buf1, every, [], nxt_j=2 * jj + 2, nxt=buf0, nxt_streams=every)
        return carry

    def tile(i, carry):
        lax.fori_loop(0, i, body, 0)
        scores(2 * i + 1, *buf1, late)
        load_queries(jnp.minimum(i + 1, n_tiles - 1))
        fused_step(2 * i, buf0, late, early, nxt_j=0, nxt=buf0, nxt_streams=every)
        fused_step(2 * i + 1, buf1, [], late, key_shift=T)
        row0 = pl.multiple_of(i * tq, tq)
        for c in range(n_chunks):
            ot = []
            for hh in range(2):
                acc = acc_scr[hh * n_chunks + c]
                ot.append(acc[:HEAD_DIM] / acc[HEAD_DIM:HEAD_DIM + 1])
            o_ref[0, pl.ds(row0 + c * Q_CHUNK, Q_CHUNK), :] = jnp.concatenate(ot, axis=0).T.astype(bf16)
        reset_stats()
        return carry

    load_queries(0)
    reset_stats()
    scores(0, *buf0, every)
    lax.fori_loop(0, n_tiles, tile, 0)


def _attn(qt, qat, k, ka, vt, *, tk=512):
    B, S, D = k.shape
    n_pairs = D // LANES
    tq = 2 * tk
    assert S % tq == 0 and tk % Q_CHUNK == 0 and LANES == 2 * HEAD_DIM
    n_streams = 2 * (tq // Q_CHUNK)
    tblk = pl.BlockSpec((1, LANES, S), lambda b, p: (b, p, 0))
    kblk = pl.BlockSpec((1, S, LANES), lambda b, p: (b, 0, p))
    return pl.pallas_call(
        functools.partial(_attn_kernel, tq=tq, tk=tk),
        grid=(B, n_pairs),
        in_specs=[tblk, tblk, kblk, kblk, tblk],
        out_specs=kblk,
        out_shape=jax.ShapeDtypeStruct((B, S, D), bf16),
        scratch_shapes=[
            pltpu.VMEM((S, 2 * LANES), bf16),
            pltpu.VMEM((2, V_ROWS, S), bf16),
            pltpu.VMEM((2, 2 * LANES, tq), bf16),
            pltpu.VMEM((n_streams, V_ROWS, Q_CHUNK), f32),
            pltpu.VMEM((n_streams, 1, Q_CHUNK), f32),
        ] + 2 * [pltpu.VMEM((n_streams, tk, Q_CHUNK), f32), pltpu.VMEM((n_streams, 1, Q_CHUNK), f32)],
        compiler_params=pltpu.CompilerParams(
            dimension_semantics=("parallel", "parallel"), vmem_limit_bytes=VMEM_LIMIT),
        name="attn",
    )(qt, qat, k, ka, vt)


def kernel(x, mix_norm, ffn_norm, pool_w, pool_scale, kv_norm, w_k, w_v, w_f, b_f, w_q, w_o,
           router_g, router_g_b, router_e, router_e_b, w_gate, w_up, w_down, final_norm):
    B, S, D = x.shape
    assert mix_norm.shape[0] == 2 and pool_w.shape[0] == 1 and w_q.shape[0] == 1

    def moe(h, layer, **kw):
        return _moe(h, ffn_norm[layer], router_g[layer], router_g_b[layer], router_e[layer],
                    router_e_b[layer], w_gate[layer], w_up[layer], w_down[layer], **kw)

    h = _mix0(x, mix_norm[0], pool_w[0], pool_scale[0])
    h = moe(h.reshape(B * S, D), 0)
    qt, qat, k, ka, vt = _kvq(h.reshape(B, S, D), kv_norm, mix_norm[1], w_k, w_v, w_q[0], w_f, b_f)
    o = _attn(qt, qat, k, ka, vt)
    out = moe(h, 1, attn=o.reshape(B * S, D), w_o=w_o[0], final_gain=final_norm)
    return out.reshape(B, S, D)
```

```python
import functools
import math

import numpy as np
import jax
import jax.numpy as jnp
from jax import lax
from jax.experimental import pallas as pl
from jax.experimental.pallas import tpu as pltpu

POOL_WINDOWS = (2, 4, 8, 16)
HEAD_DIM = 64
N_GROUPS = 4
EXPERTS_PER_GROUP = 4
EPS = 1e-6
NEG_BIG = -1e30
LOG2E = math.log2(math.e)

LANES = 128
POOL_HALO = 16
VMEM_LIMIT = 56 * 1024 * 1024

F_PIECES = 3
AUG_PER_HEAD = 2 * F_PIECES
ONE_LANE = 3 * 16

f32 = jnp.float32
bf16 = jnp.bfloat16


def _rms_inv(x):
    return lax.rsqrt(jnp.mean(x * x, axis=-1, keepdims=True) + EPS)


def _split3(x):
    hi = x.astype(bf16)
    r1 = x - hi.astype(f32)
    mid = r1.astype(bf16)
    lo = (r1 - mid.astype(f32)).astype(bf16)
    return hi, mid, lo


def _mix0_kernel(x_ref, halo_ref, g_ref, w_ref, sc_ref, o_ref, *, tile, group_width):
    t = pl.program_id(1)
    x = x_ref[0]
    g = g_ref[...]
    xn = x * _rms_inv(x) * g
    hal = halo_ref[0]
    hn = hal * _rms_inv(hal) * g * (t > 0).astype(f32)
    ext = jnp.concatenate([hn, xn], axis=0)
    pos1 = t * tile + lax.broadcasted_iota(jnp.int32, (tile, 1), 0) + 1
    outs = []
    for gi, w in enumerate(POOL_WINDOWS):
        e = ext[:, gi * group_width:(gi + 1) * group_width]
        s = e
        sh = 1
        while sh < w:
            s = s + pltpu.roll(s, sh, axis=0)
            sh *= 2
        cnt = jnp.minimum(pos1, w).astype(f32)
        diff = (s[POOL_HALO:] / cnt - e[POOL_HALO:]).astype(bf16)
        outs.append(jnp.dot(diff, w_ref[gi], preferred_element_type=f32))
    y = jnp.concatenate(outs, axis=1) * sc_ref[...]
    o_ref[0] = x + y


def _mix0(x, gain, pool_w, pool_scale, *, tile=512):
    B, S, D = x.shape
    G = len(POOL_WINDOWS)
    gw = D // G
    assert all(w & (w - 1) == 0 and w <= POOL_HALO for w in POOL_WINDOWS)
    assert S % tile == 0 and tile % POOL_HALO == 0
    hb = tile // POOL_HALO
    return pl.pallas_call(
        functools.partial(_mix0_kernel, tile=tile, group_width=gw),
        grid=(B, S // tile),
        in_specs=[
            pl.BlockSpec((1, tile, D), lambda b, t: (b, t, 0)),
            pl.BlockSpec((1, POOL_HALO, D), lambda b, t: (b, jnp.maximum(t * hb - 1, 0), 0)),
            pl.BlockSpec((1, D), lambda b, t: (0, 0)),
            pl.BlockSpec((G, gw, gw), lambda b, t: (0, 0, 0)),
            pl.BlockSpec((1, D), lambda b, t: (0, 0)),
        ],
        out_specs=pl.BlockSpec((1, tile, D), lambda b, t: (b, t, 0)),
        out_shape=jax.ShapeDtypeStruct((B, S, D), f32),
        compiler_params=pltpu.CompilerParams(
            dimension_semantics=("parallel", "parallel"), vmem_limit_bytes=VMEM_LIMIT),
        name="mix0",
    )(x, x, gain.reshape(1, D), pool_w.astype(bf16), pool_scale.reshape(1, D))


ROUTE_ROWS = 32


def _route(logits_t):
    G, E = N_GROUPS, EXPERTS_PER_GROUP
    row = lax.broadcasted_iota(jnp.int32, logits_t.shape, 0).astype(f32)
    far = float(ROUTE_ROWS)
    is_g = row < G
    gl = jnp.where(is_g, logits_t, -jnp.inf)
    gmax = jnp.max(gl, axis=0, keepdims=True)
    gidx = jnp.min(jnp.where(gl == gmax, row, far), axis=0, keepdims=True)
    gsum = jnp.sum(jnp.where(is_g, jnp.exp(logits_t - gmax), 0.0), axis=0, keepdims=True)
    g_w = 1.0 / gsum
    lo = G + E * gidx
    el = jnp.where((row >= lo) & (row < lo + E), logits_t, -jnp.inf)
    m1 = jnp.max(el, axis=0, keepdims=True)
    i1 = jnp.min(jnp.where(el == m1, row, far), axis=0, keepdims=True)
    el2 = jnp.where(row == i1, -jnp.inf, el)
    m2 = jnp.max(el2, axis=0, keepdims=True)
    i2 = jnp.min(jnp.where(el2 == m2, row, far), axis=0, keepdims=True)
    e2 = jnp.exp(m2 - m1)
    w1 = 1.0 / (1.0 + e2)
    w2 = e2 * w1
    gates_t = jnp.where(row == i1, w1 * g_w, jnp.where(row == i2, w2 * g_w, 0.0))
    return gates_t, gidx


def _experts(xb, gate_cols, wg_ref, wu_ref, wd_ref, grp):
    G, E = N_GROUPS, EXPERTS_PER_GROUP
    y = None
    for e in range(E):
        eg = E * grp + e
        hg = jnp.dot(xb, wg_ref[eg], preferred_element_type=f32)
        hu = jnp.dot(xb, wu_ref[eg], preferred_element_type=f32)
        act = (hg * (1.0 / (1.0 + jnp.exp(-hg))) * hu * gate_cols[:, G + eg:G + eg + 1]).astype(bf16)
        ye = jnp.dot(act, wd_ref[eg], preferred_element_type=f32)
        y = ye if y is None else y + ye
    return y


def _moe_kernel(*refs, with_attn, with_final, cap):
    refs = list(refs)
    h_ref = refs.pop(0)
    if with_attn:
        a_ref = refs.pop(0)
        wo_ref = refs.pop(0)
    g_ref, wr_ref, br_ref, utri_ref, wg_ref, wu_ref, wd_ref = refs[:7]
    refs = refs[7:]
    if with_final:
        fg_ref = refs.pop(0)
    (out_ref,) = refs
    T = h_ref.shape[0]
    G = N_GROUPS

    x = h_ref[...]
    if with_attn:
        x = x + jnp.dot(a_ref[...], wo_ref[...], preferred_element_type=f32)
    out_ref[...] = x
    hn = x * _rms_inv(x) * g_ref[...]
    h1 = hn.astype(bf16)
    h2 = (hn - h1.astype(f32)).astype(bf16)
    nt = (((1,), (1,)), ((), ()))
    l1 = lax.dot_general(wr_ref[...], h1, nt, preferred_element_type=f32)
    logits_t = (l1[:ROUTE_ROWS] + l1[ROUTE_ROWS:]
                + lax.dot_general(wr_ref[:ROUTE_ROWS], h2, nt, preferred_element_type=f32)
                + br_ref[...])
    gates_t, gidx = _route(logits_t)

    grow = lax.broadcasted_iota(jnp.int32, (16, T), 0).astype(f32)
    onehot = jnp.where(grow == gidx, 1.0, 0.0)
    before = jnp.dot(onehot.astype(bf16), utri_ref[...], preferred_element_type=f32)
    rank = jnp.sum(before * onehot, axis=0, keepdims=True)

    placed = rank < cap
    slot = jnp.where(placed, gidx * cap + rank, -1.0)
    sel_t = jnp.where(slot == lax.broadcasted_iota(jnp.int32, (G * cap, T), 0).astype(f32),
                      1.0, 0.0).astype(bf16)

    def token_major(cols_t):
        padded = jnp.concatenate([cols_t, jnp.zeros((LANES - ROUTE_ROWS, T), f32)], axis=0)
        return padded.T

    gates = token_major(gates_t)
    g_hi = gates.astype(bf16)
    g_lo = (gates - g_hi.astype(f32)).astype(bf16)
    xs = jnp.dot(sel_t, h1, preferred_element_type=f32).astype(bf16)
    gs2 = jnp.dot(sel_t, jnp.concatenate([g_hi, g_lo], axis=1), preferred_element_type=f32)
    gs = gs2[:, :LANES] + gs2[:, LANES:]
    E = EXPERTS_PER_GROUP
    ys = [None] * G
    pending = None
    for k in range(G * E):
        grp = k // E
        xg = xs[grp * cap:(grp + 1) * cap]
        hg = jnp.dot(xg, wg_ref[k], preferred_element_type=f32)
        hu = jnp.dot(xg, wu_ref[k], preferred_element_type=f32)
        if pending is not None:
            pending()
        gate = gs[grp * cap:(grp + 1) * cap, G + k:G + k + 1]
        act = (hg * (1.0 / (1.0 + jnp.exp(-hg))) * hu * gate).astype(bf16)

        def pending(k=k, grp=grp, act=act):
            ye = jnp.dot(act, wd_ref[k], preferred_element_type=f32)
            ys[grp] = ye if ys[grp] is None else ys[grp] + ye
    pending()
    ys = jnp.concatenate([y.astype(bf16) for y in ys], axis=0)
    tn = (((0,), (0,)), ((), ()))
    out_ref[...] += lax.dot_general(sel_t, ys, tn, preferred_element_type=f32)

    for grp in range(G):
        count = jnp.sum(onehot[grp:grp + 1, :])

        @pl.when(count > cap)
        def _(grp=grp):
            left_gates = token_major(jnp.where(placed, 0.0, gates_t))
            out_ref[...] += _experts(h1, left_gates, wg_ref, wu_ref, wd_ref, grp)

    if with_final:
        o = out_ref[...]
        out_ref[...] = o * _rms_inv(o) * fg_ref[...]


def _moe(h, ffn_gain, wg, bg, we, be, w_gate, w_up, w_down, *, attn=None, w_o=None,
         final_gain=None, tile=512, cap=160):
    N, D = h.shape
    G, E = N_GROUPS, EXPERTS_PER_GROUP
    F = w_gate.shape[-1]
    assert N % tile == 0 and G + G * E <= ROUTE_ROWS and cap % 16 == 0 and cap <= tile
    with_attn = attn is not None
    with_final = final_gain is not None

    pad = ROUTE_ROWS - G - G * E
    wr = jnp.concatenate([wg.T, we.T, jnp.zeros((pad, D), f32)], axis=0)
    wr_hi = wr.astype(bf16)
    wr_lo = (wr - wr_hi.astype(f32)).astype(bf16)
    br = jnp.concatenate([bg, be, jnp.zeros((pad,), f32)]).reshape(ROUTE_ROWS, 1)
    utri = jnp.asarray(np.triu(np.ones((tile, tile), np.float32), 1), bf16)

    tok = lambda i: (i, 0)
    const2 = lambda i: (0, 0)
    const3 = lambda i: (0, 0, 0)

    def resident(shape):
        return pl.BlockSpec(shape, const2 if len(shape) == 2 else const3,
                            pipeline_mode=pl.Buffered(1))

    in_specs = [pl.BlockSpec((tile, D), tok)]
    args = [h]
    if with_attn:
        in_specs += [pl.BlockSpec((tile, D), tok), resident((D, D))]
        args += [attn, w_o.astype(bf16)]
    in_specs += [
        resident((1, D)), resident((2 * ROUTE_ROWS, D)), resident((ROUTE_ROWS, 1)),
        resident((tile, tile)),
        resident((G * E, D, F)), resident((G * E, D, F)), resident((G * E, F, D)),
    ]
    args += [ffn_gain.reshape(1, D), jnp.concatenate([wr_hi, wr_lo], axis=0), br, utri,
             w_gate.astype(bf16), w_up.astype(bf16), w_down.astype(bf16)]
    if with_final:
        in_specs.append(resident((1, D)))
        args.append(final_gain.reshape(1, D))

    return pl.pallas_call(
        functools.partial(_moe_kernel, with_attn=with_attn, with_final=with_final, cap=cap),
        grid=(N // tile,),
        in_specs=in_specs,
        out_specs=pl.BlockSpec((tile, D), tok),
        out_shape=jax.ShapeDtypeStruct((N, D), f32),
        compiler_params=pltpu.CompilerParams(
            dimension_semantics=("parallel",), vmem_limit_bytes=VMEM_LIMIT),
        name="moe_final" if with_final else "moe",
    )(*args)


def _kvq_kernel(h_ref, gkv_ref, gq_ref, wk_ref, wv_ref, wq_ref, wf_ref, bf_ref, tri_ref,
                selq_ref, selk_ref, qt_ref, qat_ref, k_ref, ka_ref, vt_ref, carry_scr, *, q_scale):
    t = pl.program_id(1)

    @pl.when(t == 0)
    def _():
        carry_scr[...] = jnp.zeros_like(carry_scr)

    x = h_ref[0]
    xn = x * _rms_inv(x)
    hk = (xn * gkv_ref[...]).astype(bf16)
    hq = (xn * gq_ref[...]).astype(bf16)
    z = jnp.dot(hk, wf_ref[...], preferred_element_type=f32) + bf_ref[...]
    k_ref[0] = jnp.dot(hk, wk_ref[...], preferred_element_type=f32).astype(bf16)
    lane = lax.broadcasted_iota(jnp.int32, z.shape, 1)
    log_f = jnp.minimum(z, 0.0) - jnp.log(1.0 + jnp.exp(-jnp.abs(z)))
    lf2 = jnp.where(lane < ONE_LANE, log_f * LOG2E, 0.0)
    hi, mid, lo = _split3(lf2)
    tri = tri_ref[...]
    F = (jnp.dot(tri, hi, preferred_element_type=f32)
         + jnp.dot(tri, mid, preferred_element_type=f32)
         + jnp.dot(tri, lo, preferred_element_type=f32)
         + carry_scr[0:1, :])
    carry_scr[0:1, :] = F[-1:, :]
    nt = (((1,), (1,)), ((), ()))
    vt_ref[0] = lax.dot_general(wv_ref[...], hk, nt, preferred_element_type=f32).astype(bf16)
    fh, fm, fl = _split3(F)
    packed = jnp.where(lane < 16, fh, jnp.where(lane < 32, fm, jnp.where(lane < ONE_LANE, fl,
                       jnp.where(lane == ONE_LANE, 1.0, 0.0).astype(bf16))))
    qat_ref[0] = lax.dot_general(selq_ref[...], packed, nt, preferred_element_type=f32).astype(bf16)
    ka_ref[0] = jnp.dot(packed, selk_ref[...], preferred_element_type=f32).astype(bf16)
    qt_ref[0] = (lax.dot_general(wq_ref[...], hq, nt, preferred_element_type=f32) * q_scale).astype(bf16)


def _aug_selectors(n_heads):
    n_pairs = n_heads // 2
    selq = np.zeros((LANES, n_pairs * LANES), np.float32)
    selk = np.zeros((LANES, n_pairs * LANES), np.float32)
    for h in range(n_heads):
        base = (h // 2) * LANES + AUG_PER_HEAD * (h % 2)
        for p in range(F_PIECES):
            selq[h + 16 * p, base + p] = 1.0
            selq[ONE_LANE, base + F_PIECES + p] = 1.0
            selk[ONE_LANE, base + p] = 1.0
            selk[h + 16 * p, base + F_PIECES + p] = -1.0
    return jnp.asarray(selq.T, bf16), jnp.asarray(selk, bf16)


def _kvq(h, kv_gain, q_gain, w_k, w_v, w_q, w_f, b_f, *, tile=512):
    B, S, D = h.shape
    H = w_f.shape[1]
    assert H == 16 and D == H * HEAD_DIM and S % tile == 0
    wf = jnp.concatenate([w_f] * F_PIECES + [jnp.zeros((D, LANES - F_PIECES * H), f32)], axis=1)
    bfp = jnp.concatenate([b_f] * F_PIECES + [jnp.zeros((LANES - F_PIECES * H,), f32)]).reshape(1, LANES)
    tri = jnp.asarray(np.tril(np.ones((tile, tile), np.float32)), bf16)
    selq, selk = _aug_selectors(H)
    q_scale = HEAD_DIM ** -0.5 * LOG2E

    const = lambda b, t: (0, 0)
    blk = pl.BlockSpec((1, tile, D), lambda b, t: (b, t, 0))
    out = jax.ShapeDtypeStruct((B, S, D), bf16)
    blk_t = pl.BlockSpec((1, D, tile), lambda b, t: (b, 0, t))
    out_t = jax.ShapeDtypeStruct((B, D, S), bf16)
    return pl.pallas_call(
        functools.partial(_kvq_kernel, q_scale=q_scale),
        grid=(B, S // tile),
        in_specs=[
            blk,
            pl.BlockSpec((1, D), const), pl.BlockSpec((1, D), const),
            pl.BlockSpec((D, D), const), pl.BlockSpec((D, D), const), pl.BlockSpec((D, D), const),
            pl.BlockSpec((D, LANES), const), pl.BlockSpec((1, LANES), const),
            pl.BlockSpec((tile, tile), const),
            pl.BlockSpec((D // 2 // HEAD_DIM * LANES, LANES), const),
            pl.BlockSpec((LANES, D // 2 // HEAD_DIM * LANES), const),
        ],
        out_specs=[blk_t, blk_t, blk, blk, blk_t],
        out_shape=[out_t, out_t, out, out, out_t],
        scratch_shapes=[pltpu.VMEM((8, LANES), f32)],
        compiler_params=pltpu.CompilerParams(
            dimension_semantics=("parallel", "arbitrary"), vmem_limit_bytes=VMEM_LIMIT),
        name="kvq",
    )(h, kv_gain.reshape(1, D), q_gain.reshape(1, D), w_k.astype(bf16), w_v.T.astype(bf16),
      w_q.T.astype(bf16), wf.astype(bf16), bfp, tri, selq, selk)


V_ROWS = HEAD_DIM + 16
Q_CHUNK = 256
PV_LAG = 1


def _attn_kernel(qt_ref, qat_ref, k_ref, ka_ref, vt_ref, o_ref, kf_scr, vt_scr, qf_scr, acc_scr, m_scr,
                 st0_scr, mx0_scr, st1_scr, mx1_scr, *, tq, tk):
    T = tk
    n_chunks = tq // Q_CHUNK
    half = n_chunks // 2
    S = kf_scr.shape[0]
    n_tiles = S // tq

    kf_scr[:, :LANES] = k_ref[0]
    kf_scr[:, LANES:] = ka_ref[0]
    r = lax.broadcasted_iota(jnp.int32, (V_ROWS - HEAD_DIM, S), 0)
    ones_rows = jnp.where(r == 0, 1.0, 0.0).astype(bf16)
    for hh in range(2):
        vt_scr[hh, :HEAD_DIM, :] = vt_ref[0, hh * HEAD_DIM:(hh + 1) * HEAD_DIM, :]
        vt_scr[hh, HEAD_DIM:, :] = ones_rows

    feat = lax.broadcasted_iota(jnp.int32, (LANES, tq), 0)
    zero = jnp.zeros((LANES, tq), bf16)
    key = lax.broadcasted_iota(jnp.int32, (T, Q_CHUNK), 0)
    qry = lax.broadcasted_iota(jnp.int32, (T, Q_CHUNK), 1)
    streams = [(hh, c) for hh in range(2) for c in range(n_chunks)]
    every = list(range(len(streams)))
    early = [n for n in every if streams[n][1] < half]
    late = [n for n in every if streams[n][1] >= half]

    def load_queries(i):
        cols = pl.ds(pl.multiple_of(i * tq, tq), tq)
        qt = qt_ref[0, :, cols]
        qat = qat_ref[0, :, cols]
        for hh in range(2):
            qf_scr[hh, :LANES, :] = jnp.where(
                (feat >= HEAD_DIM * hh) & (feat < HEAD_DIM * (hh + 1)), qt, zero)
            qf_scr[hh, LANES:, :] = jnp.where(
                (feat >= AUG_PER_HEAD * hh) & (feat < AUG_PER_HEAD * (hh + 1)), qat, zero)

    def reset_stats():
        m_scr[...] = jnp.full(m_scr.shape, NEG_BIG, f32)
        acc_scr[...] = jnp.zeros(acc_scr.shape, f32)

    def scores(j, st_buf, mx_buf, which):
        kb = kf_scr[pl.ds(pl.multiple_of(j * T, T), T), :]
        for n in which:
            hh, c = streams[n]
            qc = qf_scr[hh, :, c * Q_CHUNK:(c + 1) * Q_CHUNK]
            st = jnp.dot(kb, qc, preferred_element_type=f32)
            st_buf[n] = st
            mx_buf[n] = jnp.max(st, axis=0, keepdims=True)

    def interleave(updates, products):
        pending = []
        for idx in range(max(len(updates), len(products))):
            if idx < len(updates):
                n, j, (st_c, mx_c), key_shift = updates[idx]
                hh, c = streams[n]
                st = st_c[n]
                if key_shift is not None:
                    st = jnp.where(key + key_shift <= qry + c * Q_CHUNK, st, NEG_BIG)
                    mx = jnp.max(st, axis=0, keepdims=True)
                else:
                    mx = mx_c[n]
                m_prev = m_scr[n]
                m_new = jnp.maximum(m_prev, mx)
                alpha = jnp.exp2(m_prev - m_new)
                pt = jnp.exp2(st - m_new).astype(bf16)
                m_scr[n] = m_new

                def value_product(n=n, hh=hh, j=j, pt=pt, alpha=alpha):
                    vt = vt_scr[hh, :, pl.ds(pl.multiple_of(j * T, T), T)]
                    acc_scr[n] = acc_scr[n] * alpha + jnp.dot(vt, pt, preferred_element_type=f32)
                pending.append(value_product)
            if idx < len(products):
                n, j, (st_n, mx_n), before = products[idx]
                if before is not None:
                    before()
                hh, c = streams[n]
                kb = kf_scr[pl.ds(pl.multiple_of(j * T, T), T), :]
                qc = qf_scr[hh, :, c * Q_CHUNK:(c + 1) * Q_CHUNK]
                st = jnp.dot(kb, qc, preferred_element_type=f32)
                st_n[n] = st
                mx_n[n] = jnp.max(st, axis=0, keepdims=True)
            if len(pending) > PV_LAG + 1:
                pending.pop(0)()
        for value_product in pending:
            value_product()

    buf0 = (st0_scr, mx0_scr)
    buf1 = (st1_scr, mx1_scr)

    def body(jj, carry):
        for j, cur, nxt in ((2 * jj, buf0, buf1), (2 * jj + 1, buf1, buf0)):
            interleave([(n, j, cur, None) for n in every], [(n, j + 1, nxt, None) for n in every])
        return carry

    def tile(i, carry):
        lax.fori_loop(0, i, body, 0)
        d0, d1 = 2 * i, 2 * i + 1
        updates = ([(n, d0, buf0, 0 if n in early else None) for n in every]
                   + [(n, d1, buf1, T) for n in late])
        next_queries = functools.partial(load_queries, jnp.minimum(i + 1, n_tiles - 1))
        products = ([(n, d1, buf1, None) for n in late]
                    + [(n, 0, buf0, next_queries if n == every[0] else None) for n in every])
        interleave(updates, products)
        row0 = pl.multiple_of(i * tq, tq)
        for c in range(n_chunks):
            ot = []
            for hh in range(2):
                acc = acc_scr[hh * n_chunks + c]
                ot.append(acc[:HEAD_DIM] / acc[HEAD_DIM:HEAD_DIM + 1])
            o_ref[0, pl.ds(row0 + c * Q_CHUNK, Q_CHUNK), :] = jnp.concatenate(ot, axis=0).T.astype(bf16)
        reset_stats()
        return carry

    load_queries(0)
    reset_stats()
    scores(0, *buf0, every)
    lax.fori_loop(0, n_tiles, tile, 0)


def _attn(qt, qat, k, ka, vt, *, tk=512):
    B, S, D = k.shape
    n_pairs = D // LANES
    tq = 2 * tk
    assert S % tq == 0 and tk % Q_CHUNK == 0 and LANES == 2 * HEAD_DIM
    n_streams = 2 * (tq // Q_CHUNK)
    tblk = pl.BlockSpec((1, LANES, S), lambda b, p: (b, p, 0))
    kblk = pl.BlockSpec((1, S, LANES), lambda b, p: (b, 0, p))
    return pl.pallas_call(
        functools.partial(_attn_kernel, tq=tq, tk=tk),
        grid=(B, n_pairs),
        in_specs=[tblk, tblk, kblk, kblk, tblk],
        out_specs=kblk,
        out_shape=jax.ShapeDtypeStruct((B, S, D), bf16),
        scratch_shapes=[
            pltpu.VMEM((S, 2 * LANES), bf16),
            pltpu.VMEM((2, V_ROWS, S), bf16),
            pltpu.VMEM((2, 2 * LANES, tq), bf16),
            pltpu.VMEM((n_streams, V_ROWS, Q_CHUNK), f32),
            pltpu.VMEM((n_streams, 1, Q_CHUNK), f32),
        ] + 2 * [pltpu.VMEM((n_streams, tk, Q_CHUNK), f32), pltpu.VMEM((n_streams, 1, Q_CHUNK), f32)],
        compiler_params=pltpu.CompilerParams(
            dimension_semantics=("parallel", "parallel"), vmem_limit_bytes=VMEM_LIMIT),
        name="attn",
    )(qt, qat, k, ka, vt)


def kernel(x, mix_norm, ffn_norm, pool_w, pool_scale, kv_norm, w_k, w_v, w_f, b_f, w_q, w_o,
           router_g, router_g_b, router_e, router_e_b, w_gate, w_up, w_down, final_norm):
    B, S, D = x.shape
    assert mix_norm.shape[0] == 2 and pool_w.shape[0] == 1 and w_q.shape[0] == 1

    def moe(h, layer, **kw):
        return _moe(h, ffn_norm[layer], router_g[layer], router_g_b[layer], router_e[layer],
                    router_e_b[layer], w_gate[layer], w_up[layer], w_down[layer], **kw)

    h = _mix0(x, mix_norm[0], pool_w[0], pool_scale[0])
    h = moe(h.reshape(B * S, D), 0)
    qt, qat, k, ka, vt = _kvq(h.reshape(B, S, D), kv_norm, mix_norm[1], w_k, w_v, w_q[0], w_f, b_f)
    o = _attn(qt, qat, k, ka, vt)
    out = moe(h, 1, attn=o.reshape(B * S, D), w_o=w_o[0], final_gain=final_norm)
    return out.reshape(B, S, D)
```

```python
import functools
import math

import numpy as np
import jax
import jax.numpy as jnp
from jax import lax
from jax.experimental import pallas as pl
from jax.experimental.pallas import tpu as pltpu

POOL_WINDOWS = (2, 4, 8, 16)
HEAD_DIM = 64
N_GROUPS = 4
EXPERTS_PER_GROUP = 4
EPS = 1e-6
NEG_BIG = -1e30
LOG2E = math.log2(math.e)

LANES = 128
POOL_HALO = 16
VMEM_LIMIT = 56 * 1024 * 1024

F_PIECES = 3
AUG_PER_HEAD = 2 * F_PIECES
ONE_ROW = F_PIECES * 16

f32 = jnp.float32
bf16 = jnp.bfloat16


def _rms_inv(x):
    return lax.rsqrt(jnp.mean(x * x, axis=-1, keepdims=True) + EPS)


def _split3(x):
    hi = x.astype(bf16)
    r1 = x - hi.astype(f32)
    mid = r1.astype(bf16)
    lo = (r1 - mid.astype(f32)).astype(bf16)
    return hi, mid, lo


def _mix0_kernel(x_ref, halo_ref, g_ref, w_ref, sc_ref, o_ref, *, tile, group_width):
    t = pl.program_id(1)
    x = x_ref[0]
    g = g_ref[...]
    xn = x * _rms_inv(x) * g
    hal = halo_ref[0]
    hn = hal * _rms_inv(hal) * g * (t > 0).astype(f32)
    ext = jnp.concatenate([hn, xn], axis=0)
    pos1 = t * tile + lax.broadcasted_iota(jnp.int32, (tile, 1), 0) + 1
    outs = []
    for gi, w in enumerate(POOL_WINDOWS):
        e = ext[:, gi * group_width:(gi + 1) * group_width]
        s = e
        sh = 1
        while sh < w:
            s = s + pltpu.roll(s, sh, axis=0)
            sh *= 2
        cnt = jnp.minimum(pos1, w).astype(f32)
        diff = (s[POOL_HALO:] / cnt - e[POOL_HALO:]).astype(bf16)
        outs.append(jnp.dot(diff, w_ref[gi], preferred_element_type=f32))
    y = jnp.concatenate(outs, axis=1) * sc_ref[...]
    o_ref[0] = x + y


def _mix0(x, gain, pool_w, pool_scale, *, tile=512):
    B, S, D = x.shape
    G = len(POOL_WINDOWS)
    gw = D // G
    assert all(w & (w - 1) == 0 and w <= POOL_HALO for w in POOL_WINDOWS)
    assert S % tile == 0 and tile % POOL_HALO == 0
    hb = tile // POOL_HALO
    return pl.pallas_call(
        functools.partial(_mix0_kernel, tile=tile, group_width=gw),
        grid=(B, S // tile),
        in_specs=[
            pl.BlockSpec((1, tile, D), lambda b, t: (b, t, 0)),
            pl.BlockSpec((1, POOL_HALO, D), lambda b, t: (b, jnp.maximum(t * hb - 1, 0), 0)),
            pl.BlockSpec((1, D), lambda b, t: (0, 0)),
            pl.BlockSpec((G, gw, gw), lambda b, t: (0, 0, 0)),
            pl.BlockSpec((1, D), lambda b, t: (0, 0)),
        ],
        out_specs=pl.BlockSpec((1, tile, D), lambda b, t: (b, t, 0)),
        out_shape=jax.ShapeDtypeStruct((B, S, D), f32),
        compiler_params=pltpu.CompilerParams(
            dimension_semantics=("parallel", "parallel"), vmem_limit_bytes=VMEM_LIMIT),
        name="mix0",
    )(x, x, gain.reshape(1, D), pool_w.astype(bf16), pool_scale.reshape(1, D))


ROUTE_ROWS = 32


def _route(logits_t):
    G, E = N_GROUPS, EXPERTS_PER_GROUP
    row = lax.broadcasted_iota(jnp.int32, logits_t.shape, 0).astype(f32)
    far = float(ROUTE_ROWS)
    is_g = row < G
    gl = jnp.where(is_g, logits_t, -jnp.inf)
    gmax = jnp.max(gl, axis=0, keepdims=True)
    gidx = jnp.min(jnp.where(gl == gmax, row, far), axis=0, keepdims=True)
    gsum = jnp.sum(jnp.where(is_g, jnp.exp(logits_t - gmax), 0.0), axis=0, keepdims=True)
    g_w = 1.0 / gsum
    lo = G + E * gidx
    el = jnp.where((row >= lo) & (row < lo + E), logits_t, -jnp.inf)
    m1 = jnp.max(el, axis=0, keepdims=True)
    i1 = jnp.min(jnp.where(el == m1, row, far), axis=0, keepdims=True)
    el2 = jnp.where(row == i1, -jnp.inf, el)
    m2 = jnp.max(el2, axis=0, keepdims=True)
    i2 = jnp.min(jnp.where(el2 == m2, row, far), axis=0, keepdims=True)
    e2 = jnp.exp(m2 - m1)
    w1 = 1.0 / (1.0 + e2)
    w2 = e2 * w1
    gates_t = jnp.where(row == i1, w1 * g_w, jnp.where(row == i2, w2 * g_w, 0.0))
    return gates_t, gidx


def _experts(xb, gate_cols, wg_ref, wu_ref, wd_ref, grp):
    G, E = N_GROUPS, EXPERTS_PER_GROUP
    y = None
    for e in range(E):
        eg = E * grp + e
        hg = jnp.dot(xb, wg_ref[eg], preferred_element_type=f32)
        hu = jnp.dot(xb, wu_ref[eg], preferred_element_type=f32)
        act = (hg * (1.0 / (1.0 + jnp.exp(-hg))) * hu * gate_cols[:, G + eg:G + eg + 1]).astype(bf16)
        ye = jnp.dot(act, wd_ref[eg], preferred_element_type=f32)
        y = ye if y is None else y + ye
    return y


def _moe_kernel(*refs, with_attn, with_final, cap):
    refs = list(refs)
    h_ref = refs.pop(0)
    if with_attn:
        a_ref = refs.pop(0)
        wo_ref = refs.pop(0)
    g_ref, wr_ref, br_ref, utri_ref, wg_ref, wu_ref, wd_ref = refs[:7]
    refs = refs[7:]
    if with_final:
        fg_ref = refs.pop(0)
    (out_ref,) = refs
    T = h_ref.shape[0]
    G = N_GROUPS

    x = h_ref[...]
    if with_attn:
        x = x + jnp.dot(a_ref[...], wo_ref[...], preferred_element_type=f32)
    out_ref[...] = x
    hn = x * _rms_inv(x) * g_ref[...]
    h1 = hn.astype(bf16)
    h2 = (hn - h1.astype(f32)).astype(bf16)
    nt = (((1,), (1,)), ((), ()))
    l1 = lax.dot_general(wr_ref[...], h1, nt, preferred_element_type=f32)
    logits_t = (l1[:ROUTE_ROWS] + l1[ROUTE_ROWS:]
                + lax.dot_general(wr_ref[:ROUTE_ROWS], h2, nt, preferred_element_type=f32)
                + br_ref[...])
    gates_t, gidx = _route(logits_t)

    grow = lax.broadcasted_iota(jnp.int32, (16, T), 0).astype(f32)
    onehot = jnp.where(grow == gidx, 1.0, 0.0)
    before = jnp.dot(onehot.astype(bf16), utri_ref[...], preferred_element_type=f32)
    rank = jnp.sum(before * onehot, axis=0, keepdims=True)

    placed = rank < cap
    slot = jnp.where(placed, gidx * cap + rank, -1.0)
    sel_t = jnp.where(slot == lax.broadcasted_iota(jnp.int32, (G * cap, T), 0).astype(f32),
                      1.0, 0.0).astype(bf16)

    def token_major(cols_t):
        padded = jnp.concatenate([cols_t, jnp.zeros((LANES - ROUTE_ROWS, T), f32)], axis=0)
        return padded.T

    gates = token_major(gates_t)
    g_hi = gates.astype(bf16)
    g_lo = (gates - g_hi.astype(f32)).astype(bf16)
    xs = jnp.dot(sel_t, h1, preferred_element_type=f32).astype(bf16)
    gs2 = jnp.dot(sel_t, jnp.concatenate([g_hi, g_lo], axis=1), preferred_element_type=f32)
    gs = gs2[:, :LANES] + gs2[:, LANES:]
    E = EXPERTS_PER_GROUP
    ys = [None] * G
    pending = None
    for k in range(G * E):
        grp = k // E
        xg = xs[grp * cap:(grp + 1) * cap]
        hg = jnp.dot(xg, wg_ref[k], preferred_element_type=f32)
        hu = jnp.dot(xg, wu_ref[k], preferred_element_type=f32)
        if pending is not None:
            pending()
        gate = gs[grp * cap:(grp + 1) * cap, G + k:G + k + 1]
        act = (hg * (1.0 / (1.0 + jnp.exp(-hg))) * hu * gate).astype(bf16)

        def pending(k=k, grp=grp, act=act):
            ye = jnp.dot(act, wd_ref[k], preferred_element_type=f32)
            ys[grp] = ye if ys[grp] is None else ys[grp] + ye
    pending()
    ys = jnp.concatenate([y.astype(bf16) for y in ys], axis=0)
    tn = (((0,), (0,)), ((), ()))
    out_ref[...] += lax.dot_general(sel_t, ys, tn, preferred_element_type=f32)

    for grp in range(G):
        count = jnp.sum(onehot[grp:grp + 1, :])

        @pl.when(count > cap)
        def _(grp=grp):
            left_gates = token_major(jnp.where(placed, 0.0, gates_t))
            out_ref[...] += _experts(h1, left_gates, wg_ref, wu_ref, wd_ref, grp)

    if with_final:
        o = out_ref[...]
        out_ref[...] = o * _rms_inv(o) * fg_ref[...]


def _moe(h, ffn_gain, wg, bg, we, be, w_gate, w_up, w_down, *, attn=None, w_o=None,
         final_gain=None, tile=512, cap=160):
    N, D = h.shape
    G, E = N_GROUPS, EXPERTS_PER_GROUP
    F = w_gate.shape[-1]
    assert N % tile == 0 and G + G * E <= ROUTE_ROWS and cap % 16 == 0 and cap <= tile
    with_attn = attn is not None
    with_final = final_gain is not None

    pad = ROUTE_ROWS - G - G * E
    wr = jnp.concatenate([wg.T, we.T, jnp.zeros((pad, D), f32)], axis=0)
    wr_hi = wr.astype(bf16)
    wr_lo = (wr - wr_hi.astype(f32)).astype(bf16)
    br = jnp.concatenate([bg, be, jnp.zeros((pad,), f32)]).reshape(ROUTE_ROWS, 1)
    utri = jnp.asarray(np.triu(np.ones((tile, tile), np.float32), 1), bf16)

    tok = lambda i: (i, 0)
    const2 = lambda i: (0, 0)
    const3 = lambda i: (0, 0, 0)

    def resident(shape):
        return pl.BlockSpec(shape, const2 if len(shape) == 2 else const3,
                            pipeline_mode=pl.Buffered(1))

    in_specs = [pl.BlockSpec((tile, D), tok)]
    args = [h]
    if with_attn:
        in_specs += [pl.BlockSpec((tile, D), tok), resident((D, D))]
        args += [attn, w_o.astype(bf16)]
    in_specs += [
        resident((1, D)), resident((2 * ROUTE_ROWS, D)), resident((ROUTE_ROWS, 1)),
        resident((tile, tile)),
        resident((G * E, D, F)), resident((G * E, D, F)), resident((G * E, F, D)),
    ]
    args += [ffn_gain.reshape(1, D), jnp.concatenate([wr_hi, wr_lo], axis=0), br, utri,
             w_gate.astype(bf16), w_up.astype(bf16), w_down.astype(bf16)]
    if with_final:
        in_specs.append(resident((1, D)))
        args.append(final_gain.reshape(1, D))

    return pl.pallas_call(
        functools.partial(_moe_kernel, with_attn=with_attn, with_final=with_final, cap=cap),
        grid=(N // tile,),
        in_specs=in_specs,
        out_specs=pl.BlockSpec((tile, D), tok),
        out_shape=jax.ShapeDtypeStruct((N, D), f32),
        compiler_params=pltpu.CompilerParams(
            dimension_semantics=("parallel",), vmem_limit_bytes=VMEM_LIMIT),
        name="moe_final" if with_final else "moe",
    )(*args)


def _kvq_kernel(h_ref, gkv_ref, gq_ref, wk_ref, wv_ref, wq_ref, wf_ref, bf_ref, tri_ref,
                selq_ref, selk_ref, qt_ref, qat_ref, k_ref, ka_ref, vt_ref, carry_scr, *, q_scale):
    t = pl.program_id(1)

    @pl.when(t == 0)
    def _():
        carry_scr[...] = jnp.zeros_like(carry_scr)

    x = h_ref[0]
    xn = x * _rms_inv(x)
    hk = (xn * gkv_ref[...]).astype(bf16)
    hq = (xn * gq_ref[...]).astype(bf16)
    nt = (((1,), (1,)), ((), ()))
    tn = (((0,), (0,)), ((), ()))
    T = x.shape[0]
    z = lax.dot_general(wf_ref[...], hk, nt, preferred_element_type=f32) + bf_ref[...]
    k_ref[0] = jnp.dot(hk, wk_ref[...], preferred_element_type=f32).astype(bf16)
    lf2 = (jnp.minimum(z, 0.0) - jnp.log(1.0 + jnp.exp(-jnp.abs(z)))) * LOG2E
    c = jnp.dot(jnp.concatenate(_split3(lf2), axis=0), tri_ref[...], preferred_element_type=f32)
    H = z.shape[0]
    F = c[:H] + c[H:2 * H] + c[2 * H:] + carry_scr[:, 0:1]
    carry_scr[...] = jnp.broadcast_to(F[:, T - 1:T], carry_scr.shape)
    vt_ref[0] = lax.dot_general(wv_ref[...], hk, nt, preferred_element_type=f32).astype(bf16)
    one_row = jnp.where(lax.broadcasted_iota(jnp.int32, (H, T), 0) == 0, 1.0, 0.0).astype(bf16)
    packed = jnp.concatenate(_split3(F) + (one_row,), axis=0)
    qat_ref[0] = jnp.dot(selq_ref[...], packed, preferred_element_type=f32).astype(bf16)
    ka_ref[0] = lax.dot_general(packed, selk_ref[...], tn, preferred_element_type=f32).astype(bf16)
    qt_ref[0] = (lax.dot_general(wq_ref[...], hq, nt, preferred_element_type=f32) * q_scale).astype(bf16)


def _aug_selectors(n_heads):
    n_pairs = n_heads // 2
    rows = (F_PIECES + 1) * n_heads
    selq = np.zeros((rows, n_pairs * LANES), np.float32)
    selk = np.zeros((rows, n_pairs * LANES), np.float32)
    for h in range(n_heads):
        base = (h // 2) * LANES + AUG_PER_HEAD * (h % 2)
        for p in range(F_PIECES):
            selq[h + n_heads * p, base + p] = 1.0
            selq[ONE_ROW, base + F_PIECES + p] = 1.0
            selk[ONE_ROW, base + p] = 1.0
            selk[h + n_heads * p, base + F_PIECES + p] = -1.0
    return jnp.asarray(selq.T, bf16), jnp.asarray(selk, bf16)


def _kvq(h, kv_gain, q_gain, w_k, w_v, w_q, w_f, b_f, *, tile=512):
    B, S, D = h.shape
    H = w_f.shape[1]
    assert H == 16 and D == H * HEAD_DIM and S % tile == 0
    tri = jnp.asarray(np.triu(np.ones((tile, tile), np.float32)), bf16)
    selq, selk = _aug_selectors(H)
    aug_rows = (F_PIECES + 1) * H
    q_scale = HEAD_DIM ** -0.5 * LOG2E

    const = lambda b, t: (0, 0)
    blk = pl.BlockSpec((1, tile, D), lambda b, t: (b, t, 0))
    out = jax.ShapeDtypeStruct((B, S, D), bf16)
    blk_t = pl.BlockSpec((1, D, tile), lambda b, t: (b, 0, t))
    out_t = jax.ShapeDtypeStruct((B, D, S), bf16)
    return pl.pallas_call(
        functools.partial(_kvq_kernel, q_scale=q_scale),
        grid=(B, S // tile),
        in_specs=[
            blk,
            pl.BlockSpec((1, D), const), pl.BlockSpec((1, D), const),
            pl.BlockSpec((D, D), const), pl.BlockSpec((D, D), const), pl.BlockSpec((D, D), const),
            pl.BlockSpec((H, D), const), pl.BlockSpec((H, 1), const),
            pl.BlockSpec((tile, tile), const),
            pl.BlockSpec((H // 2 * LANES, aug_rows), const),
            pl.BlockSpec((aug_rows, H // 2 * LANES), const),
        ],
        out_specs=[blk_t, blk_t, blk, blk, blk_t],
        out_shape=[out_t, out_t, out, out, out_t],
        scratch_shapes=[pltpu.VMEM((H, LANES), f32)],
        compiler_params=pltpu.CompilerParams(
            dimension_semantics=("parallel", "arbitrary"), vmem_limit_bytes=VMEM_LIMIT),
        name="kvq",
    )(h, kv_gain.reshape(1, D), q_gain.reshape(1, D), w_k.astype(bf16), w_v.T.astype(bf16),
      w_q.T.astype(bf16), w_f.T.astype(bf16), b_f.reshape(H, 1), tri, selq, selk)


V_ROWS = HEAD_DIM + 16
Q_CHUNK = 256
PV_LAG = 1


def _attn_kernel(qt_ref, qat_ref, k_ref, ka_ref, vt_ref, o_ref, kf_scr, vt_scr, qf_scr, acc_scr, m_scr,
                 st0_scr, mx0_scr, st1_scr, mx1_scr, *, tq, tk):
    T = tk
    n_chunks = tq // Q_CHUNK
    half = n_chunks // 2
    S = kf_scr.shape[0]
    n_tiles = S // tq

    kf_scr[:, :LANES] = k_ref[0]
    kf_scr[:, LANES:] = ka_ref[0]
    r = lax.broadcasted_iota(jnp.int32, (V_ROWS - HEAD_DIM, S), 0)
    ones_rows = jnp.where(r == 0, 1.0, 0.0).astype(bf16)
    for hh in range(2):
        vt_scr[hh, :HEAD_DIM, :] = vt_ref[0, hh * HEAD_DIM:(hh + 1) * HEAD_DIM, :]
        vt_scr[hh, HEAD_DIM:, :] = ones_rows

    feat = lax.broadcasted_iota(jnp.int32, (LANES, tq), 0)
    zero = jnp.zeros((LANES, tq), bf16)
    key = lax.broadcasted_iota(jnp.int32, (T, Q_CHUNK), 0)
    qry = lax.broadcasted_iota(jnp.int32, (T, Q_CHUNK), 1)
    streams = [(hh, c) for hh in range(2) for c in range(n_chunks)]
    every = list(range(len(streams)))
    early = [n for n in every if streams[n][1] < half]
    late = [n for n in every if streams[n][1] >= half]

    def load_queries(i):
        cols = pl.ds(pl.multiple_of(i * tq, tq), tq)
        qt = qt_ref[0, :, cols]
        qat = qat_ref[0, :, cols]
        for hh in range(2):
            qf_scr[hh, :LANES, :] = jnp.where(
                (feat >= HEAD_DIM * hh) & (feat < HEAD_DIM * (hh + 1)), qt, zero)
            qf_scr[hh, LANES:, :] = jnp.where(
                (feat >= AUG_PER_HEAD * hh) & (feat < AUG_PER_HEAD * (hh + 1)), qat, zero)

    def reset_stats():
        m_scr[...] = jnp.full(m_scr.shape, NEG_BIG, f32)
        acc_scr[...] = jnp.zeros(acc_scr.shape, f32)

    def scores(j, st_buf, mx_buf, which):
        kb = kf_scr[pl.ds(pl.multiple_of(j * T, T), T), :]
        for n in which:
            hh, c = streams[n]
            qc = qf_scr[hh, :, c * Q_CHUNK:(c + 1) * Q_CHUNK]
            st = jnp.dot(kb, qc, preferred_element_type=f32)
            st_buf[n] = st
            mx_buf[n] = jnp.max(st, axis=0, keepdims=True)

    def interleave(updates, products):
        pending = []
        for idx in range(max(len(updates), len(products))):
            if idx < len(updates):
                n, j, (st_c, mx_c), key_shift = updates[idx]
                hh, c = streams[n]
                st = st_c[n]
                if key_shift is not None:
                    st = jnp.where(key + key_shift <= qry + c * Q_CHUNK, st, NEG_BIG)
                    mx = jnp.max(st, axis=0, keepdims=True)
                else:
                    mx = mx_c[n]
                m_prev = m_scr[n]
                m_new = jnp.maximum(m_prev, mx)
                alpha = jnp.exp2(m_prev - m_new)
                pt = jnp.exp2(st - m_new).astype(bf16)
                m_scr[n] = m_new

                def value_product(n=n, hh=hh, j=j, pt=pt, alpha=alpha):
                    vt = vt_scr[hh, :, pl.ds(pl.multiple_of(j * T, T), T)]
                    acc_scr[n] = acc_scr[n] * alpha + jnp.dot(vt, pt, preferred_element_type=f32)
                pending.append(value_product)
            if idx < len(products):
                n, j, (st_n, mx_n), before = products[idx]
                if before is not None:
                    before()
                hh, c = streams[n]
                kb = kf_scr[pl.ds(pl.multiple_of(j * T, T), T), :]
                qc = qf_scr[hh, :, c * Q_CHUNK:(c + 1) * Q_CHUNK]
                st = jnp.dot(kb, qc, preferred_element_type=f32)
                st_n[n] = st
                mx_n[n] = jnp.max(st, axis=0, keepdims=True)
            if len(pending) > PV_LAG + 1:
                pending.pop(0)()
        for value_product in pending:
            value_product()

    buf0 = (st0_scr, mx0_scr)
    buf1 = (st1_scr, mx1_scr)

    def body(jj, carry):
        j = 2 * jj
        interleave([(n, j, buf0, None) for n in every] + [(n, j + 1, buf1, None) for n in every],
                   [(n, j + 1, buf1, None) for n in every] + [(n, j + 2, buf0, None) for n in every])
        return carry

    def tile(i, carry, has_next=True):
        lax.fori_loop(0, i, body, 0)
        d0, d1 = 2 * i, 2 * i + 1
        updates = ([(n, d0, buf0, 0 if n in early else None) for n in every]
                   + [(n, d1, buf1, T) for n in late])
        products = [(n, d1, buf1, None) for n in late]
        if has_next:
            next_queries = functools.partial(load_queries, i + 1)
            products += [(n, 0, buf0, next_queries if n == every[0] else None) for n in every]
        interleave(updates, products)
        row0 = pl.multiple_of(i * tq, tq)
        for c in range(n_chunks):
            ot = []
            for hh in range(2):
                acc = acc_scr[hh * n_chunks + c]
                ot.append(acc[:HEAD_DIM] / acc[HEAD_DIM:HEAD_DIM + 1])
            o_ref[0, pl.ds(row0 + c * Q_CHUNK, Q_CHUNK), :] = jnp.concatenate(ot, axis=0).T.astype(bf16)
        reset_stats()
        return carry

    load_queries(0)
    reset_stats()
    scores(0, *buf0, every)
    lax.fori_loop(0, n_tiles - 1, tile, 0)
    tile(n_tiles - 1, 0, has_next=False)


def _attn(qt, qat, k, ka, vt, *, tk=512):
    B, S, D = k.shape
    n_pairs = D // LANES
    tq = 2 * tk
    assert S % tq == 0 and tk % Q_CHUNK == 0 and LANES == 2 * HEAD_DIM
    n_streams = 2 * (tq // Q_CHUNK)
    tblk = pl.BlockSpec((1, LANES, S), lambda b, p: (b, p, 0))
    kblk = pl.BlockSpec((1, S, LANES), lambda b, p: (b, 0, p))
    return pl.pallas_call(
        functools.partial(_attn_kernel, tq=tq, tk=tk),
        grid=(B, n_pairs),
        in_specs=[tblk, tblk, kblk, kblk, tblk],
        out_specs=kblk,
        out_shape=jax.ShapeDtypeStruct((B, S, D), bf16),
        scratch_shapes=[
            pltpu.VMEM((S, 2 * LANES), bf16),
            pltpu.VMEM((2, V_ROWS, S), bf16),
            pltpu.VMEM((2, 2 * LANES, tq), bf16),
            pltpu.VMEM((n_streams, V_ROWS, Q_CHUNK), f32),
            pltpu.VMEM((n_streams, 1, Q_CHUNK), f32),
        ] + 2 * [pltpu.VMEM((n_streams, tk, Q_CHUNK), f32), pltpu.VMEM((n_streams, 1, Q_CHUNK), f32)],
        compiler_params=pltpu.CompilerParams(
            dimension_semantics=("parallel", "parallel"), vmem_limit_bytes=VMEM_LIMIT),
        name="attn",
    )(qt, qat, k, ka, vt)


def kernel(x, mix_norm, ffn_norm, pool_w, pool_scale, kv_norm, w_k, w_v, w_f, b_f, w_q, w_o,
           router_g, router_g_b, router_e, router_e_b, w_gate, w_up, w_down, final_norm):
    B, S, D = x.shape
    assert mix_norm.shape[0] == 2 and pool_w.shape[0] == 1 and w_q.shape[0] == 1

    def moe(h, layer, **kw):
        return _moe(h, ffn_norm[layer], router_g[layer], router_g_b[layer], router_e[layer],
                    router_e_b[layer], w_gate[layer], w_up[layer], w_down[layer], **kw)

    h = _mix0(x, mix_norm[0], pool_w[0], pool_scale[0])
    h = moe(h.reshape(B * S, D), 0)
    qt, qat, k, ka, vt = _kvq(h.reshape(B, S, D), kv_norm, mix_norm[1], w_k, w_v, w_q[0], w_f, b_f)
    o = _attn(qt, qat, k, ka, vt)
    out = moe(h, 1, attn=o.reshape(B * S, D), w_o=w_o[0], final_gain=final_norm)
    return out.reshape(B, S, D)
```

```python
import functools
import math

import numpy as np
import jax
import jax.numpy as jnp
from jax import lax
from jax.experimental import pallas as pl
from jax.experimental.pallas import tpu as pltpu

POOL_WINDOWS = (2, 4, 8, 16)
HEAD_DIM = 64
N_GROUPS = 4
EXPERTS_PER_GROUP = 4
EPS = 1e-6
NEG_BIG = -1e30
LOG2E = math.log2(math.e)

LANES = 128
POOL_HALO = 16
VMEM_LIMIT = 56 * 1024 * 1024

F_PIECES = 3
AUG_PER_HEAD = 2 * F_PIECES
ONE_ROW = F_PIECES * 16

f32 = jnp.float32
bf16 = jnp.bfloat16


def _rms_inv(x):
    return lax.rsqrt(jnp.mean(x * x, axis=-1, keepdims=True) + EPS)


def _split3(x):
    hi = x.astype(bf16)
    r1 = x - hi.astype(f32)
    mid = r1.astype(bf16)
    lo = (r1 - mid.astype(f32)).astype(bf16)
    return hi, mid, lo


def _mix0_kernel(x_ref, halo_ref, g_ref, w_ref, sc_ref, o_ref, *, tile, group_width):
    t = pl.program_id(1)
    x = x_ref[0]
    g = g_ref[...]
    xn = x * _rms_inv(x) * g
    hal = halo_ref[0]
    hn = hal * _rms_inv(hal) * g * (t > 0).astype(f32)
    ext = jnp.concatenate([hn, xn], axis=0)
    pos1 = t * tile + lax.broadcasted_iota(jnp.int32, (tile, 1), 0) + 1
    outs = []
    for gi, w in enumerate(POOL_WINDOWS):
        e = ext[:, gi * group_width:(gi + 1) * group_width]
        s = e
        sh = 1
        while sh < w:
            s = s + pltpu.roll(s, sh, axis=0)
            sh *= 2
        cnt = jnp.minimum(pos1, w).astype(f32)
        diff = (s[POOL_HALO:] / cnt - e[POOL_HALO:]).astype(bf16)
        outs.append(jnp.dot(diff, w_ref[gi], preferred_element_type=f32))
    y = jnp.concatenate(outs, axis=1) * sc_ref[...]
    o_ref[0] = x + y


def _mix0(x, gain, pool_w, pool_scale, *, tile=512):
    B, S, D = x.shape
    G = len(POOL_WINDOWS)
    gw = D // G
    assert all(w & (w - 1) == 0 and w <= POOL_HALO for w in POOL_WINDOWS)
    assert S % tile == 0 and tile % POOL_HALO == 0
    hb = tile // POOL_HALO
    return pl.pallas_call(
        functools.partial(_mix0_kernel, tile=tile, group_width=gw),
        grid=(B, S // tile),
        in_specs=[
            pl.BlockSpec((1, tile, D), lambda b, t: (b, t, 0)),
            pl.BlockSpec((1, POOL_HALO, D), lambda b, t: (b, jnp.maximum(t * hb - 1, 0), 0)),
            pl.BlockSpec((1, D), lambda b, t: (0, 0)),
            pl.BlockSpec((G, gw, gw), lambda b, t: (0, 0, 0)),
            pl.BlockSpec((1, D), lambda b, t: (0, 0)),
        ],
        out_specs=pl.BlockSpec((1, tile, D), lambda b, t: (b, t, 0)),
        out_shape=jax.ShapeDtypeStruct((B, S, D), f32),
        compiler_params=pltpu.CompilerParams(
            dimension_semantics=("parallel", "parallel"), vmem_limit_bytes=VMEM_LIMIT),
        name="mix0",
    )(x, x, gain.reshape(1, D), pool_w.astype(bf16), pool_scale.reshape(1, D))


ROUTE_ROWS = 32


def _route(logits_t):
    G, E = N_GROUPS, EXPERTS_PER_GROUP
    row = lax.broadcasted_iota(jnp.int32, logits_t.shape, 0).astype(f32)
    far = float(ROUTE_ROWS)
    is_g = row < G
    gl = jnp.where(is_g, logits_t, -jnp.inf)
    gmax = jnp.max(gl, axis=0, keepdims=True)
    gidx = jnp.min(jnp.where(gl == gmax, row, far), axis=0, keepdims=True)
    gsum = jnp.sum(jnp.where(is_g, jnp.exp(logits_t - gmax), 0.0), axis=0, keepdims=True)
    g_w = 1.0 / gsum
    lo = G + E * gidx
    el = jnp.where((row >= lo) & (row < lo + E), logits_t, -jnp.inf)
    m1 = jnp.max(el, axis=0, keepdims=True)
    i1 = jnp.min(jnp.where(el == m1, row, far), axis=0, keepdims=True)
    el2 = jnp.where(row == i1, -jnp.inf, el)
    m2 = jnp.max(el2, axis=0, keepdims=True)
    i2 = jnp.min(jnp.where(el2 == m2, row, far), axis=0, keepdims=True)
    e2 = jnp.exp(m2 - m1)
    w1 = 1.0 / (1.0 + e2)
    w2 = e2 * w1
    gates_t = jnp.where(row == i1, w1 * g_w, jnp.where(row == i2, w2 * g_w, 0.0))
    return gates_t, gidx


def _experts(xb, gate_cols, wg_ref, wu_ref, wd_ref, grp):
    G, E = N_GROUPS, EXPERTS_PER_GROUP
    y = None
    for e in range(E):
        eg = E * grp + e
        hg = jnp.dot(xb, wg_ref[eg], preferred_element_type=f32)
        hu = jnp.dot(xb, wu_ref[eg], preferred_element_type=f32)
        act = (hg * (1.0 / (1.0 + jnp.exp(-hg))) * hu * gate_cols[:, G + eg:G + eg + 1]).astype(bf16)
        ye = jnp.dot(act, wd_ref[eg], preferred_element_type=f32)
        y = ye if y is None else y + ye
    return y


def _moe_kernel(*refs, with_attn, with_final, cap):
    refs = list(refs)
    h_ref = refs.pop(0)
    if with_attn:
        a_ref = refs.pop(0)
        wo_ref = refs.pop(0)
    g_ref, wr_ref, br_ref, utri_ref, wg_ref, wu_ref, wd_ref = refs[:7]
    refs = refs[7:]
    if with_final:
        fg_ref = refs.pop(0)
    (out_ref,) = refs
    T = h_ref.shape[0]
    G = N_GROUPS

    x = h_ref[...]
    if with_attn:
        x = x + jnp.dot(a_ref[...], wo_ref[...], preferred_element_type=f32)
    out_ref[...] = x
    hn = x * _rms_inv(x) * g_ref[...]
    h1 = hn.astype(bf16)
    nt = (((1,), (1,)), ((), ()))
    l1 = lax.dot_general(wr_ref[...], h1, nt, preferred_element_type=f32)
    logits_t = l1[:ROUTE_ROWS] + l1[ROUTE_ROWS:] + br_ref[...]
    gates_t, gidx = _route(logits_t)

    grow = lax.broadcasted_iota(jnp.int32, (16, T), 0).astype(f32)
    onehot = jnp.where(grow == gidx, 1.0, 0.0)
    before = jnp.dot(onehot.astype(bf16), utri_ref[...], preferred_element_type=f32)
    rank = jnp.sum(before * onehot, axis=0, keepdims=True)

    placed = rank < cap
    slot = jnp.where(placed, gidx * cap + rank, -1.0)
    sel_t = jnp.where(slot == lax.broadcasted_iota(jnp.int32, (G * cap, T), 0).astype(f32),
                      1.0, 0.0).astype(bf16)

    def token_major(cols_t):
        padded = jnp.concatenate([cols_t, jnp.zeros((LANES - ROUTE_ROWS, T), f32)], axis=0)
        return padded.T

    gates = token_major(gates_t)
    g_hi = gates.astype(bf16)
    g_lo = (gates - g_hi.astype(f32)).astype(bf16)
    xs = jnp.dot(sel_t, h1, preferred_element_type=f32).astype(bf16)
    gs2 = jnp.dot(sel_t, jnp.concatenate([g_hi, g_lo], axis=1), preferred_element_type=f32)
    gs = gs2[:, :LANES] + gs2[:, LANES:]
    E = EXPERTS_PER_GROUP
    ys = [None] * G
    pending = None
    for k in range(G * E):
        grp = k // E
        xg = xs[grp * cap:(grp + 1) * cap]
        hg = jnp.dot(xg, wg_ref[k], preferred_element_type=f32)
        hu = jnp.dot(xg, wu_ref[k], preferred_element_type=f32)
        if pending is not None:
            pending()
        gate = gs[grp * cap:(grp + 1) * cap, G + k:G + k + 1]
        act = (hg * (1.0 / (1.0 + jnp.exp(-hg))) * hu * gate).astype(bf16)

        def pending(k=k, grp=grp, act=act):
            ye = jnp.dot(act, wd_ref[k], preferred_element_type=f32)
            ys[grp] = ye if ys[grp] is None else ys[grp] + ye
    pending()
    ys = jnp.concatenate([y.astype(bf16) for y in ys], axis=0)
    tn = (((0,), (0,)), ((), ()))
    out_ref[...] += lax.dot_general(sel_t, ys, tn, preferred_element_type=f32)

    for grp in range(G):
        count = jnp.sum(onehot[grp:grp + 1, :])

        @pl.when(count > cap)
        def _(grp=grp):
            left_gates = token_major(jnp.where(placed, 0.0, gates_t))
            out_ref[...] += _experts(h1, left_gates, wg_ref, wu_ref, wd_ref, grp)

    if with_final:
        o = out_ref[...]
        out_ref[...] = o * _rms_inv(o) * fg_ref[...]


def _moe(h, ffn_gain, wg, bg, we, be, w_gate, w_up, w_down, *, attn=None, w_o=None,
         final_gain=None, tile=512, cap=160):
    N, D = h.shape
    G, E = N_GROUPS, EXPERTS_PER_GROUP
    F = w_gate.shape[-1]
    assert N % tile == 0 and G + G * E <= ROUTE_ROWS and cap % 16 == 0 and cap <= tile
    with_attn = attn is not None
    with_final = final_gain is not None

    pad = ROUTE_ROWS - G - G * E
    wr = jnp.concatenate([wg.T, we.T, jnp.zeros((pad, D), f32)], axis=0)
    wr_hi = wr.astype(bf16)
    wr_lo = (wr - wr_hi.astype(f32)).astype(bf16)
    br = jnp.concatenate([bg, be, jnp.zeros((pad,), f32)]).reshape(ROUTE_ROWS, 1)
    utri = jnp.asarray(np.triu(np.ones((tile, tile), np.float32), 1), bf16)

    tok = lambda i: (i, 0)
    const2 = lambda i: (0, 0)
    const3 = lambda i: (0, 0, 0)

    def resident(shape):
        return pl.BlockSpec(shape, const2 if len(shape) == 2 else const3,
                            pipeline_mode=pl.Buffered(1))

    in_specs = [pl.BlockSpec((tile, D), tok)]
    args = [h]
    if with_attn:
        in_specs += [pl.BlockSpec((tile, D), tok), resident((D, D))]
        args += [attn, w_o.astype(bf16)]
    in_specs += [
        resident((1, D)), resident((2 * ROUTE_ROWS, D)), resident((ROUTE_ROWS, 1)),
        resident((tile, tile)),
        resident((G * E, D, F)), resident((G * E, D, F)), resident((G * E, F, D)),
    ]
    args += [ffn_gain.reshape(1, D), jnp.concatenate([wr_hi, wr_lo], axis=0), br, utri,
             w_gate.astype(bf16), w_up.astype(bf16), w_down.astype(bf16)]
    if with_final:
        in_specs.append(resident((1, D)))
        args.append(final_gain.reshape(1, D))

    return pl.pallas_call(
        functools.partial(_moe_kernel, with_attn=with_attn, with_final=with_final, cap=cap),
        grid=(N // tile,),
        in_specs=in_specs,
        out_specs=pl.BlockSpec((tile, D), tok),
        out_shape=jax.ShapeDtypeStruct((N, D), f32),
        compiler_params=pltpu.CompilerParams(
            dimension_semantics=("parallel",), vmem_limit_bytes=VMEM_LIMIT),
        name="moe_final" if with_final else "moe",
    )(*args)


def _kvq_kernel(h_ref, gkv_ref, gq_ref, wk_ref, wv_ref, wq_ref, wf_ref, bf_ref, tri_ref,
                selq_ref, selk_ref, qt_ref, qat_ref, k_ref, ka_ref, vt_ref, carry_scr, *, q_scale):
    t = pl.program_id(1)

    @pl.when(t == 0)
    def _():
        carry_scr[...] = jnp.zeros_like(carry_scr)

    x = h_ref[0]
    xn = x * _rms_inv(x)
    hk = (xn * gkv_ref[...]).astype(bf16)
    hq = (xn * gq_ref[...]).astype(bf16)
    nt = (((1,), (1,)), ((), ()))
    tn = (((0,), (0,)), ((), ()))
    T = x.shape[0]
    z = lax.dot_general(wf_ref[...], hk, nt, preferred_element_type=f32) + bf_ref[...]
    k_ref[0] = jnp.dot(hk, wk_ref[...], preferred_element_type=f32).astype(bf16)
    lf2 = (jnp.minimum(z, 0.0) - jnp.log(1.0 + jnp.exp(-jnp.abs(z)))) * LOG2E
    c = jnp.dot(jnp.concatenate(_split3(lf2), axis=0), tri_ref[...], preferred_element_type=f32)
    H = z.shape[0]
    F = c[:H] + c[H:2 * H] + c[2 * H:] + carry_scr[:, 0:1]
    carry_scr[...] = jnp.broadcast_to(F[:, T - 1:T], carry_scr.shape)
    vt_ref[0] = lax.dot_general(wv_ref[...], hk, nt, preferred_element_type=f32).astype(bf16)
    one_row = jnp.where(lax.broadcasted_iota(jnp.int32, (H, T), 0) == 0, 1.0, 0.0).astype(bf16)
    packed = jnp.concatenate(_split3(F) + (one_row,), axis=0)
    qat_ref[0] = jnp.dot(selq_ref[...], packed, preferred_element_type=f32).astype(bf16)
    ka_ref[0] = lax.dot_general(packed, selk_ref[...], tn, preferred_element_type=f32).astype(bf16)
    qt_ref[0] = (lax.dot_general(wq_ref[...], hq, nt, preferred_element_type=f32) * q_scale).astype(bf16)


def _aug_selectors(n_heads):
    n_pairs = n_heads // 2
    rows = (F_PIECES + 1) * n_heads
    selq = np.zeros((rows, n_pairs * LANES), np.float32)
    selk = np.zeros((rows, n_pairs * LANES), np.float32)
    for h in range(n_heads):
        base = (h // 2) * LANES + AUG_PER_HEAD * (h % 2)
        for p in range(F_PIECES):
            selq[h + n_heads * p, base + p] = 1.0
            selq[ONE_ROW, base + F_PIECES + p] = 1.0
            selk[ONE_ROW, base + p] = 1.0
            selk[h + n_heads * p, base + F_PIECES + p] = -1.0
    return jnp.asarray(selq.T, bf16), jnp.asarray(selk, bf16)


def _kvq(h, kv_gain, q_gain, w_k, w_v, w_q, w_f, b_f, *, tile=512):
    B, S, D = h.shape
    H = w_f.shape[1]
    assert H == 16 and D == H * HEAD_DIM and S % tile == 0
    tri = jnp.asarray(np.triu(np.ones((tile, tile), np.float32)), bf16)
    selq, selk = _aug_selectors(H)
    aug_rows = (F_PIECES + 1) * H
    q_scale = HEAD_DIM ** -0.5 * LOG2E

    const = lambda b, t: (0, 0)
    blk = pl.BlockSpec((1, tile, D), lambda b, t: (b, t, 0))
    out = jax.ShapeDtypeStruct((B, S, D), bf16)
    blk_t = pl.BlockSpec((1, D, tile), lambda b, t: (b, 0, t))
    out_t = jax.ShapeDtypeStruct((B, D, S), bf16)
    return pl.pallas_call(
        functools.partial(_kvq_kernel, q_scale=q_scale),
        grid=(B, S // tile),
        in_specs=[
            blk,
            pl.BlockSpec((1, D), const), pl.BlockSpec((1, D), const),
            pl.BlockSpec((D, D), const), pl.BlockSpec((D, D), const), pl.BlockSpec((D, D), const),
            pl.BlockSpec((H, D), const), pl.BlockSpec((H, 1), const),
            pl.BlockSpec((tile, tile), const),
            pl.BlockSpec((H // 2 * LANES, aug_rows), const),
            pl.BlockSpec((aug_rows, H // 2 * LANES), const),
        ],
        out_specs=[blk_t, blk_t, blk, blk, blk_t],
        out_shape=[out_t, out_t, out, out, out_t],
        scratch_shapes=[pltpu.VMEM((H, LANES), f32)],
        compiler_params=pltpu.CompilerParams(
            dimension_semantics=("parallel", "arbitrary"), vmem_limit_bytes=VMEM_LIMIT),
        name="kvq",
    )(h, kv_gain.reshape(1, D), q_gain.reshape(1, D), w_k.astype(bf16), w_v.T.astype(bf16),
      w_q.T.astype(bf16), w_f.T.astype(bf16), b_f.reshape(H, 1), tri, selq, selk)


V_ROWS = HEAD_DIM + 16
Q_CHUNK = 256
PV_LAG = 1


def _attn_kernel(qt_ref, qat_ref, k_ref, ka_ref, vt_ref, o_ref, kf_scr, vt_scr, qf_scr, acc_scr, m_scr,
                 st0_scr, mx0_scr, st1_scr, mx1_scr, *, tq, tk):
    T = tk
    n_chunks = tq // Q_CHUNK
    half = n_chunks // 2
    S = kf_scr.shape[0]
    n_tiles = S // tq

    kf_scr[:, :LANES] = k_ref[0]
    kf_scr[:, LANES:] = ka_ref[0]
    r = lax.broadcasted_iota(jnp.int32, (V_ROWS - HEAD_DIM, S), 0)
    ones_rows = jnp.where(r == 0, 1.0, 0.0).astype(bf16)
    for hh in range(2):
        vt_scr[hh, :HEAD_DIM, :] = vt_ref[0, hh * HEAD_DIM:(hh + 1) * HEAD_DIM, :]
        vt_scr[hh, HEAD_DIM:, :] = ones_rows

    feat = lax.broadcasted_iota(jnp.int32, (LANES, tq), 0)
    zero = jnp.zeros((LANES, tq), bf16)
    key = lax.broadcasted_iota(jnp.int32, (T, Q_CHUNK), 0)
    qry = lax.broadcasted_iota(jnp.int32, (T, Q_CHUNK), 1)
    streams = [(hh, c) for hh in range(2) for c in range(n_chunks)]
    every = list(range(len(streams)))
    early = [n for n in every if streams[n][1] < half]
    late = [n for n in every if streams[n][1] >= half]

    def load_queries(i):
        cols = pl.ds(pl.multiple_of(i * tq, tq), tq)
        qt = qt_ref[0, :, cols]
        qat = qat_ref[0, :, cols]
        for hh in range(2):
            qf_scr[hh, :LANES, :] = jnp.where(
                (feat >= HEAD_DIM * hh) & (feat < HEAD_DIM * (hh + 1)), qt, zero)
            qf_scr[hh, LANES:, :] = jnp.where(
                (feat >= AUG_PER_HEAD * hh) & (feat < AUG_PER_HEAD * (hh + 1)), qat, zero)

    def reset_stats():
        m_scr[...] = jnp.full(m_scr.shape, NEG_BIG, f32)
        acc_scr[...] = jnp.zeros(acc_scr.shape, f32)

    def scores(j, st_buf, mx_buf, which):
        kb = kf_scr[pl.ds(pl.multiple_of(j * T, T), T), :]
        for n in which:
            hh, c = streams[n]
            qc = qf_scr[hh, :, c * Q_CHUNK:(c + 1) * Q_CHUNK]
            st = jnp.dot(kb, qc, preferred_element_type=f32)
            st_buf[n] = st
            mx_buf[n] = jnp.max(st, axis=0, keepdims=True)

    def interleave(updates, products):
        pending = []
        for idx in range(max(len(updates), len(products))):
            if idx < len(updates):
                n, j, (st_c, mx_c), key_shift = updates[idx]
                hh, c = streams[n]
                st = st_c[n]
                if key_shift is not None:
                    st = jnp.where(key + key_shift <= qry + c * Q_CHUNK, st, NEG_BIG)
                    mx = jnp.max(st, axis=0, keepdims=True)
                else:
                    mx = mx_c[n]
                m_prev = m_scr[n]
                m_new = jnp.maximum(m_prev, mx)
                alpha = jnp.exp2(m_prev - m_new)
                pt = jnp.exp2(st - m_new).astype(bf16)
                m_scr[n] = m_new

                def value_product(n=n, hh=hh, j=j, pt=pt, alpha=alpha):
                    vt = vt_scr[hh, :, pl.ds(pl.multiple_of(j * T, T), T)]
                    acc_scr[n] = acc_scr[n] * alpha + jnp.dot(vt, pt, preferred_element_type=f32)
                pending.append(value_product)
            if idx < len(products):
                n, j, (st_n, mx_n), before = products[idx]
                if before is not None:
                    before()
                hh, c = streams[n]
                kb = kf_scr[pl.ds(pl.multiple_of(j * T, T), T), :]
                qc = qf_scr[hh, :, c * Q_CHUNK:(c + 1) * Q_CHUNK]
                st = jnp.dot(kb, qc, preferred_element_type=f32)
                st_n[n] = st
                mx_n[n] = jnp.max(st, axis=0, keepdims=True)
            if len(pending) > PV_LAG + 1:
                pending.pop(0)()
        for value_product in pending:
            value_product()

    buf0 = (st0_scr, mx0_scr)
    buf1 = (st1_scr, mx1_scr)

    def body(jj, carry):
        j = 2 * jj
        interleave([(n, j, buf0, None) for n in every] + [(n, j + 1, buf1, None) for n in every],
                   [(n, j + 1, buf1, None) for n in every] + [(n, j + 2, buf0, None) for n in every])
        return carry

    def tile(i, carry, has_next=True):
        lax.fori_loop(0, i, body, 0)
        d0, d1 = 2 * i, 2 * i + 1
        updates = ([(n, d0, buf0, 0 if n in early else None) for n in every]
                   + [(n, d1, buf1, T) for n in late])
        products = [(n, d1, buf1, None) for n in late]
        if has_next:
            next_queries = functools.partial(load_queries, i + 1)
            products += [(n, 0, buf0, next_queries if n == every[0] else None) for n in every]
        interleave(updates, products)
        row0 = pl.multiple_of(i * tq, tq)
        for c in range(n_chunks):
            ot = []
            for hh in range(2):
                acc = acc_scr[hh * n_chunks + c]
                ot.append(acc[:HEAD_DIM] / acc[HEAD_DIM:HEAD_DIM + 1])
            o_ref[0, pl.ds(row0 + c * Q_CHUNK, Q_CHUNK), :] = jnp.concatenate(ot, axis=0).T.astype(bf16)
        reset_stats()
        return carry

    load_queries(0)
    reset_stats()
    scores(0, *buf0, every)
    lax.fori_loop(0, n_tiles - 1, tile, 0)
    tile(n_tiles - 1, 0, has_next=False)


def _attn(qt, qat, k, ka, vt, *, tk=512):
    B, S, D = k.shape
    n_pairs = D // LANES
    tq = 2 * tk
    assert S % tq == 0 and tk % Q_CHUNK == 0 and LANES == 2 * HEAD_DIM
    n_streams = 2 * (tq // Q_CHUNK)
    tblk = pl.BlockSpec((1, LANES, S), lambda b, p: (b, p, 0))
    kblk = pl.BlockSpec((1, S, LANES), lambda b, p: (b, 0, p))
    return pl.pallas_call(
        functools.partial(_attn_kernel, tq=tq, tk=tk),
        grid=(B, n_pairs),
        in_specs=[tblk, tblk, kblk, kblk, tblk],
        out_specs=kblk,
        out_shape=jax.ShapeDtypeStruct((B, S, D), bf16),
        scratch_shapes=[
            pltpu.VMEM((S, 2 * LANES), bf16),
            pltpu.VMEM((2, V_ROWS, S), bf16),
            pltpu.VMEM((2, 2 * LANES, tq), bf16),
            pltpu.VMEM((n_streams, V_ROWS, Q_CHUNK), f32),
            pltpu.VMEM((n_streams, 1, Q_CHUNK), f32),
        ] + 2 * [pltpu.VMEM((n_streams, tk, Q_CHUNK), f32), pltpu.VMEM((n_streams, 1, Q_CHUNK), f32)],
        compiler_params=pltpu.CompilerParams(
            dimension_semantics=("parallel", "parallel"), vmem_limit_bytes=VMEM_LIMIT),
        name="attn",
    )(qt, qat, k, ka, vt)


def kernel(x, mix_norm, ffn_norm, pool_w, pool_scale, kv_norm, w_k, w_v, w_f, b_f, w_q, w_o,
           router_g, router_g_b, router_e, router_e_b, w_gate, w_up, w_down, final_norm):
    B, S, D = x.shape
    assert mix_norm.shape[0] == 2 and pool_w.shape[0] == 1 and w_q.shape[0] == 1

    def moe(h, layer, **kw):
        return _moe(h, ffn_norm[layer], router_g[layer], router_g_b[layer], router_e[layer],
                    router_e_b[layer], w_gate[layer], w_up[layer], w_down[layer], **kw)

    h = _mix0(x, mix_norm[0], pool_w[0], pool_scale[0])
    h = moe(h.reshape(B * S, D), 0)
    qt, qat, k, ka, vt = _kvq(h.reshape(B, S, D), kv_norm, mix_norm[1], w_k, w_v, w_q[0], w_f, b_f)
    o = _attn(qt, qat, k, ka, vt)
    out = moe(h, 1, attn=o.reshape(B * S, D), w_o=w_o[0], final_gain=final_norm)
    return out.reshape(B, S, D)
```

```python
import functools
import math

import numpy as np
import jax
import jax.numpy as jnp
from jax import lax
from jax.experimental import pallas as pl
from jax.experimental.pallas import tpu as pltpu

POOL_WINDOWS = (2, 4, 8, 16)
HEAD_DIM = 64
N_GROUPS = 4
EXPERTS_PER_GROUP = 4
EPS = 1e-6
NEG_BIG = -1e30
LOG2E = math.log2(math.e)

LANES = 128
POOL_HALO = 16
VMEM_LIMIT = 56 * 1024 * 1024

F_PIECES = 3
AUG_PER_HEAD = 2 * F_PIECES
ONE_ROW = F_PIECES * 16

f32 = jnp.float32
bf16 = jnp.bfloat16


def _rms_inv(x):
    return lax.rsqrt(jnp.mean(x * x, axis=-1, keepdims=True) + EPS)


def _split3(x):
    hi = x.astype(bf16)
    r1 = x - hi.astype(f32)
    mid = r1.astype(bf16)
    lo = (r1 - mid.astype(f32)).astype(bf16)
    return hi, mid, lo


def _mix0_kernel(x_ref, halo_ref, g_ref, w_ref, sc_ref, o_ref, *, tile, group_width):
    t = pl.program_id(1)
    x = x_ref[0]
    g = g_ref[...]
    xn = x * _rms_inv(x) * g
    hal = halo_ref[0]
    hn = hal * _rms_inv(hal) * g * (t > 0).astype(f32)
    ext = jnp.concatenate([hn, xn], axis=0)
    pos1 = t * tile + lax.broadcasted_iota(jnp.int32, (tile, 1), 0) + 1
    outs = []
    for gi, w in enumerate(POOL_WINDOWS):
        e = ext[:, gi * group_width:(gi + 1) * group_width]
        s = e
        sh = 1
        while sh < w:
            s = s + pltpu.roll(s, sh, axis=0)
            sh *= 2
        cnt = jnp.minimum(pos1, w).astype(f32)
        diff = (s[POOL_HALO:] / cnt - e[POOL_HALO:]).astype(bf16)
        outs.append(jnp.dot(diff, w_ref[gi], preferred_element_type=f32))
    y = jnp.concatenate(outs, axis=1) * sc_ref[...]
    o_ref[0] = x + y


def _mix0(x, gain, pool_w, pool_scale, *, tile=1024):
    B, S, D = x.shape
    G = len(POOL_WINDOWS)
    gw = D // G
    assert all(w & (w - 1) == 0 and w <= POOL_HALO for w in POOL_WINDOWS)
    assert S % tile == 0 and tile % POOL_HALO == 0
    hb = tile // POOL_HALO
    return pl.pallas_call(
        functools.partial(_mix0_kernel, tile=tile, group_width=gw),
        grid=(B, S // tile),
        in_specs=[
            pl.BlockSpec((1, tile, D), lambda b, t: (b, t, 0)),
            pl.BlockSpec((1, POOL_HALO, D), lambda b, t: (b, jnp.maximum(t * hb - 1, 0), 0)),
            pl.BlockSpec((1, D), lambda b, t: (0, 0)),
            pl.BlockSpec((G, gw, gw), lambda b, t: (0, 0, 0)),
            pl.BlockSpec((1, D), lambda b, t: (0, 0)),
        ],
        out_specs=pl.BlockSpec((1, tile, D), lambda b, t: (b, t, 0)),
        out_shape=jax.ShapeDtypeStruct((B, S, D), f32),
        compiler_params=pltpu.CompilerParams(
            dimension_semantics=("parallel", "parallel"), vmem_limit_bytes=VMEM_LIMIT),
        name="mix0",
    )(x, x, gain.reshape(1, D), pool_w.astype(bf16), pool_scale.reshape(1, D))


ROUTE_ROWS = 32


def _route(logits_t):
    G, E = N_GROUPS, EXPERTS_PER_GROUP
    row = lax.broadcasted_iota(jnp.int32, logits_t.shape, 0).astype(f32)
    far = float(ROUTE_ROWS)
    is_g = row < G
    gl = jnp.where(is_g, logits_t, -jnp.inf)
    gmax = jnp.max(gl, axis=0, keepdims=True)
    gidx = jnp.min(jnp.where(gl == gmax, row, far), axis=0, keepdims=True)
    gsum = jnp.sum(jnp.where(is_g, jnp.exp(logits_t - gmax), 0.0), axis=0, keepdims=True)
    g_w = 1.0 / gsum
    lo = G + E * gidx
    el = jnp.where((row >= lo) & (row < lo + E), logits_t, -jnp.inf)
    m1 = jnp.max(el, axis=0, keepdims=True)
    i1 = jnp.min(jnp.where(el == m1, row, far), axis=0, keepdims=True)
    el2 = jnp.where(row == i1, -jnp.inf, el)
    m2 = jnp.max(el2, axis=0, keepdims=True)
    i2 = jnp.min(jnp.where(el2 == m2, row, far), axis=0, keepdims=True)
    e2 = jnp.exp(m2 - m1)
    w1 = 1.0 / (1.0 + e2)
    w2 = e2 * w1
    gates_t = jnp.where(row == i1, w1 * g_w, jnp.where(row == i2, w2 * g_w, 0.0))
    return gates_t, gidx


def _experts(xb, gate_cols, wg_ref, wu_ref, wd_ref, grp):
    G, E = N_GROUPS, EXPERTS_PER_GROUP
    y = None
    for e in range(E):
        eg = E * grp + e
        hg = jnp.dot(xb, wg_ref[eg], preferred_element_type=f32)
        hu = jnp.dot(xb, wu_ref[eg], preferred_element_type=f32)
        act = (hg * (1.0 / (1.0 + jnp.exp(-hg))) * hu * gate_cols[:, G + eg:G + eg + 1]).astype(bf16)
        ye = jnp.dot(act, wd_ref[eg], preferred_element_type=f32)
        y = ye if y is None else y + ye
    return y


def _moe_kernel(*refs, with_attn, with_final, cap):
    refs = list(refs)
    h_ref = refs.pop(0)
    if with_attn:
        a_ref = refs.pop(0)
        wo_ref = refs.pop(0)
    g_ref, wr_ref, br_ref, utri_ref, wg_ref, wu_ref, wd_ref = refs[:7]
    refs = refs[7:]
    if with_final:
        fg_ref = refs.pop(0)
    (out_ref,) = refs
    T = h_ref.shape[0]
    G = N_GROUPS

    x = h_ref[...]
    if with_attn:
        x = x + jnp.dot(a_ref[...], wo_ref[...], preferred_element_type=f32)
    out_ref[...] = x
    hn = x * _rms_inv(x) * g_ref[...]
    h1 = hn.astype(bf16)
    nt = (((1,), (1,)), ((), ()))
    l1 = lax.dot_general(wr_ref[...], h1, nt, preferred_element_type=f32)
    logits_t = l1[:ROUTE_ROWS] + l1[ROUTE_ROWS:] + br_ref[...]
    gates_t, gidx = _route(logits_t)

    grow = lax.broadcasted_iota(jnp.int32, (16, T), 0).astype(f32)
    onehot = jnp.where(grow == gidx, 1.0, 0.0)
    before = jnp.dot(onehot.astype(bf16), utri_ref[...], preferred_element_type=f32)
    rank = jnp.sum(before * onehot, axis=0, keepdims=True)

    placed = rank < cap
    slot = jnp.where(placed, gidx * cap + rank, -1.0)
    sel_t = jnp.where(slot == lax.broadcasted_iota(jnp.int32, (G * cap, T), 0).astype(f32),
                      1.0, 0.0).astype(bf16)

    def token_major(cols_t):
        padded = jnp.concatenate([cols_t, jnp.zeros((LANES - ROUTE_ROWS, T), f32)], axis=0)
        return padded.T

    gates = token_major(gates_t)
    g_hi = gates.astype(bf16)
    g_lo = (gates - g_hi.astype(f32)).astype(bf16)
    xs = jnp.dot(sel_t, h1, preferred_element_type=f32).astype(bf16)
    gs2 = jnp.dot(sel_t, jnp.concatenate([g_hi, g_lo], axis=1), preferred_element_type=f32)
    gs = gs2[:, :LANES] + gs2[:, LANES:]
    E = EXPERTS_PER_GROUP
    ys = [None] * G
    pending = None
    for k in range(G * E):
        grp = k // E
        xg = xs[grp * cap:(grp + 1) * cap]
        hg = jnp.dot(xg, wg_ref[k], preferred_element_type=f32)
        hu = jnp.dot(xg, wu_ref[k], preferred_element_type=f32)
        if pending is not None:
            pending()
        gate = gs[grp * cap:(grp + 1) * cap, G + k:G + k + 1]
        act = (hg * (1.0 / (1.0 + jnp.exp(-hg))) * hu * gate).astype(bf16)

        def pending(k=k, grp=grp, act=act):
            ye = jnp.dot(act, wd_ref[k], preferred_element_type=f32)
            ys[grp] = ye if ys[grp] is None else ys[grp] + ye
    pending()
    ys = jnp.concatenate([y.astype(bf16) for y in ys], axis=0)
    tn = (((0,), (0,)), ((), ()))
    out_ref[...] += lax.dot_general(sel_t, ys, tn, preferred_element_type=f32)

    for grp in range(G):
        count = jnp.sum(onehot[grp:grp + 1, :])

        @pl.when(count > cap)
        def _(grp=grp):
            left_gates = token_major(jnp.where(placed, 0.0, gates_t))
            out_ref[...] += _experts(h1, left_gates, wg_ref, wu_ref, wd_ref, grp)

    if with_final:
        o = out_ref[...]
        out_ref[...] = o * _rms_inv(o) * fg_ref[...]


def _moe(h, ffn_gain, wg, bg, we, be, w_gate, w_up, w_down, *, attn=None, w_o=None,
         final_gain=None, tile=512, cap=160):
    N, D = h.shape
    G, E = N_GROUPS, EXPERTS_PER_GROUP
    F = w_gate.shape[-1]
    assert N % tile == 0 and G + G * E <= ROUTE_ROWS and cap % 16 == 0 and cap <= tile
    with_attn = attn is not None
    with_final = final_gain is not None

    pad = ROUTE_ROWS - G - G * E
    wr = jnp.concatenate([wg.T, we.T, jnp.zeros((pad, D), f32)], axis=0)
    wr_hi = wr.astype(bf16)
    wr_lo = (wr - wr_hi.astype(f32)).astype(bf16)
    br = jnp.concatenate([bg, be, jnp.zeros((pad,), f32)]).reshape(ROUTE_ROWS, 1)
    utri = jnp.asarray(np.triu(np.ones((tile, tile), np.float32), 1), bf16)

    tok = lambda i: (i, 0)
    const2 = lambda i: (0, 0)
    const3 = lambda i: (0, 0, 0)

    def resident(shape):
        return pl.BlockSpec(shape, const2 if len(shape) == 2 else const3,
                            pipeline_mode=pl.Buffered(1))

    in_specs = [pl.BlockSpec((tile, D), tok)]
    args = [h]
    if with_attn:
        in_specs += [pl.BlockSpec((tile, D), tok), resident((D, D))]
        args += [attn, w_o.astype(bf16)]
    in_specs += [
        resident((1, D)), resident((2 * ROUTE_ROWS, D)), resident((ROUTE_ROWS, 1)),
        resident((tile, tile)),
        resident((G * E, D, F)), resident((G * E, D, F)), resident((G * E, F, D)),
    ]
    args += [ffn_gain.reshape(1, D), jnp.concatenate([wr_hi, wr_lo], axis=0), br, utri,
             w_gate.astype(bf16), w_up.astype(bf16), w_down.astype(bf16)]
    if with_final:
        in_specs.append(resident((1, D)))
        args.append(final_gain.reshape(1, D))

    return pl.pallas_call(
        functools.partial(_moe_kernel, with_attn=with_attn, with_final=with_final, cap=cap),
        grid=(N // tile,),
        in_specs=in_specs,
        out_specs=pl.BlockSpec((tile, D), tok),
        out_shape=jax.ShapeDtypeStruct((N, D), f32),
        compiler_params=pltpu.CompilerParams(
            dimension_semantics=("parallel",), vmem_limit_bytes=VMEM_LIMIT),
        name="moe_final" if with_final else "moe",
    )(*args)


def _kvq_kernel(h_ref, gkv_ref, gq_ref, wk_ref, wv_ref, wq_ref, wf_ref, bf_ref, tri_ref,
                selq_ref, selk_ref, qt_ref, qat_ref, k_ref, ka_ref, vt_ref, carry_scr, *, q_scale):
    t = pl.program_id(1)

    @pl.when(t == 0)
    def _():
        carry_scr[...] = jnp.zeros_like(carry_scr)

    x = h_ref[0]
    xn = x * _rms_inv(x)
    hk = (xn * gkv_ref[...]).astype(bf16)
    hq = (xn * gq_ref[...]).astype(bf16)
    nt = (((1,), (1,)), ((), ()))
    tn = (((0,), (0,)), ((), ()))
    T = x.shape[0]
    z = lax.dot_general(wf_ref[...], hk, nt, preferred_element_type=f32) + bf_ref[...]
    k_ref[0] = jnp.dot(hk, wk_ref[...], preferred_element_type=f32).astype(bf16)
    lf2 = (jnp.minimum(z, 0.0) - jnp.log(1.0 + jnp.exp(-jnp.abs(z)))) * LOG2E
    c = jnp.dot(jnp.concatenate(_split3(lf2), axis=0), tri_ref[...], preferred_element_type=f32)
    H = z.shape[0]
    F = c[:H] + c[H:2 * H] + c[2 * H:] + carry_scr[:, 0:1]
    carry_scr[...] = jnp.broadcast_to(F[:, T - 1:T], carry_scr.shape)
    vt_ref[0] = lax.dot_general(wv_ref[...], hk, nt, preferred_element_type=f32).astype(bf16)
    one_row = jnp.where(lax.broadcasted_iota(jnp.int32, (H, T), 0) == 0, 1.0, 0.0).astype(bf16)
    packed = jnp.concatenate(_split3(F) + (one_row,), axis=0)
    qat_ref[0] = jnp.dot(selq_ref[...], packed, preferred_element_type=f32).astype(bf16)
    ka_ref[0] = lax.dot_general(packed, selk_ref[...], tn, preferred_element_type=f32).astype(bf16)
    qt_ref[0] = (lax.dot_general(wq_ref[...], hq, nt, preferred_element_type=f32) * q_scale).astype(bf16)


def _aug_selectors(n_heads):
    n_pairs = n_heads // 2
    rows = (F_PIECES + 1) * n_heads
    selq = np.zeros((rows, n_pairs * LANES), np.float32)
    selk = np.zeros((rows, n_pairs * LANES), np.float32)
    for h in range(n_heads):
        base = (h // 2) * LANES + AUG_PER_HEAD * (h % 2)
        for p in range(F_PIECES):
            selq[h + n_heads * p, base + p] = 1.0
            selq[ONE_ROW, base + F_PIECES + p] = 1.0
            selk[ONE_ROW, base + p] = 1.0
            selk[h + n_heads * p, base + F_PIECES + p] = -1.0
    return jnp.asarray(selq.T, bf16), jnp.asarray(selk, bf16)


def _kvq(h, kv_gain, q_gain, w_k, w_v, w_q, w_f, b_f, *, tile=1024):
    B, S, D = h.shape
    H = w_f.shape[1]
    assert H == 16 and D == H * HEAD_DIM and S % tile == 0
    tri = jnp.asarray(np.triu(np.ones((tile, tile), np.float32)), bf16)
    selq, selk = _aug_selectors(H)
    aug_rows = (F_PIECES + 1) * H
    q_scale = HEAD_DIM ** -0.5 * LOG2E

    const = lambda b, t: (0, 0)
    blk = pl.BlockSpec((1, tile, D), lambda b, t: (b, t, 0))
    out = jax.ShapeDtypeStruct((B, S, D), bf16)
    blk_t = pl.BlockSpec((1, D, tile), lambda b, t: (b, 0, t))
    out_t = jax.ShapeDtypeStruct((B, D, S), bf16)
    return pl.pallas_call(
        functools.partial(_kvq_kernel, q_scale=q_scale),
        grid=(B, S // tile),
        in_specs=[
            blk,
            pl.BlockSpec((1, D), const), pl.BlockSpec((1, D), const),
            pl.BlockSpec((D, D), const), pl.BlockSpec((D, D), const), pl.BlockSpec((D, D), const),
            pl.BlockSpec((H, D), const), pl.BlockSpec((H, 1), const),
            pl.BlockSpec((tile, tile), const),
            pl.BlockSpec((H // 2 * LANES, aug_rows), const),
            pl.BlockSpec((aug_rows, H // 2 * LANES), const),
        ],
        out_specs=[blk_t, blk_t, blk, blk, blk_t],
        out_shape=[out_t, out_t, out, out, out_t],
        scratch_shapes=[pltpu.VMEM((H, LANES), f32)],
        compiler_params=pltpu.CompilerParams(
            dimension_semantics=("parallel", "arbitrary"), vmem_limit_bytes=VMEM_LIMIT),
        name="kvq",
    )(h, kv_gain.reshape(1, D), q_gain.reshape(1, D), w_k.astype(bf16), w_v.T.astype(bf16),
      w_q.T.astype(bf16), w_f.T.astype(bf16), b_f.reshape(H, 1), tri, selq, selk)


V_ROWS = HEAD_DIM + 16
Q_CHUNK = 256
PV_LAG = 1


def _attn_kernel(qt_ref, qat_ref, k_ref, ka_ref, vt_ref, o_ref, kf_scr, vt_scr, qf_scr, acc_scr, m_scr,
                 st0_scr, mx0_scr, st1_scr, mx1_scr, *, tq, tk):
    T = tk
    n_chunks = tq // Q_CHUNK
    half = n_chunks // 2
    S = kf_scr.shape[0]
    n_tiles = S // tq

    kf_scr[:, :LANES] = k_ref[0]
    kf_scr[:, LANES:] = ka_ref[0]
    r = lax.broadcasted_iota(jnp.int32, (V_ROWS - HEAD_DIM, S), 0)
    ones_rows = jnp.where(r == 0, 1.0, 0.0).astype(bf16)
    for hh in range(2):
        vt_scr[hh, :HEAD_DIM, :] = vt_ref[0, hh * HEAD_DIM:(hh + 1) * HEAD_DIM, :]
        vt_scr[hh, HEAD_DIM:, :] = ones_rows

    feat = lax.broadcasted_iota(jnp.int32, (LANES, tq), 0)
    zero = jnp.zeros((LANES, tq), bf16)
    key = lax.broadcasted_iota(jnp.int32, (T, Q_CHUNK), 0)
    qry = lax.broadcasted_iota(jnp.int32, (T, Q_CHUNK), 1)
    streams = [(hh, c) for hh in range(2) for c in range(n_chunks)]
    every = list(range(len(streams)))
    early = [n for n in every if streams[n][1] < half]
    late = [n for n in every if streams[n][1] >= half]

    def load_queries(i):
        cols = pl.ds(pl.multiple_of(i * tq, tq), tq)
        qt = qt_ref[0, :, cols]
        qat = qat_ref[0, :, cols]
        for hh in range(2):
            qf_scr[hh, :LANES, :] = jnp.where(
                (feat >= HEAD_DIM * hh) & (feat < HEAD_DIM * (hh + 1)), qt, zero)
            qf_scr[hh, LANES:, :] = jnp.where(
                (feat >= AUG_PER_HEAD * hh) & (feat < AUG_PER_HEAD * (hh + 1)), qat, zero)

    def reset_stats():
        m_scr[...] = jnp.full(m_scr.shape, NEG_BIG, f32)
        acc_scr[...] = jnp.zeros(acc_scr.shape, f32)

    def scores(j, st_buf, mx_buf, which):
        kb = kf_scr[pl.ds(pl.multiple_of(j * T, T), T), :]
        for n in which:
            hh, c = streams[n]
            qc = qf_scr[hh, :, c * Q_CHUNK:(c + 1) * Q_CHUNK]
            st = jnp.dot(kb, qc, preferred_element_type=f32)
            st_buf[n] = st
            mx_buf[n] = jnp.max(st, axis=0, keepdims=True)

    def interleave(updates, products):
        pending = []
        for idx in range(max(len(updates), len(products))):
            if idx < len(updates):
                n, j, (st_c, mx_c), key_shift = updates[idx]
                hh, c = streams[n]
                st = st_c[n]
                if key_shift is not None:
                    st = jnp.where(key + key_shift <= qry + c * Q_CHUNK, st, NEG_BIG)
                    mx = jnp.max(st, axis=0, keepdims=True)
                else:
                    mx = mx_c[n]
                m_prev = m_scr[n]
                m_new = jnp.maximum(m_prev, mx)
                alpha = jnp.exp2(m_prev - m_new)
                pt = jnp.exp2(st - m_new).astype(bf16)
                m_scr[n] = m_new

                def value_product(n=n, hh=hh, j=j, pt=pt, alpha=alpha):
                    vt = vt_scr[hh, :, pl.ds(pl.multiple_of(j * T, T), T)]
                    acc_scr[n] = acc_scr[n] * alpha + jnp.dot(vt, pt, preferred_element_type=f32)
                pending.append(value_product)
            if idx < len(products):
                n, j, (st_n, mx_n), before = products[idx]
                if before is not None:
                    before()
                hh, c = streams[n]
                kb = kf_scr[pl.ds(pl.multiple_of(j * T, T), T), :]
                qc = qf_scr[hh, :, c * Q_CHUNK:(c + 1) * Q_CHUNK]
                st = jnp.dot(kb, qc, preferred_element_type=f32)
                st_n[n] = st
                mx_n[n] = jnp.max(st, axis=0, keepdims=True)
            if len(pending) > PV_LAG + 1:
                pending.pop(0)()
        for value_product in pending:
            value_product()

    buf0 = (st0_scr, mx0_scr)
    buf1 = (st1_scr, mx1_scr)

    def body(jj, carry):
        j = 2 * jj
        interleave([(n, j, buf0, None) for n in every] + [(n, j + 1, buf1, None) for n in every],
                   [(n, j + 1, buf1, None) for n in every] + [(n, j + 2, buf0, None) for n in every])
        return carry

    def tile(i, carry, has_next=True):
        lax.fori_loop(0, i, body, 0)
        d0, d1 = 2 * i, 2 * i + 1
        updates = ([(n, d0, buf0, 0 if n in early else None) for n in every]
                   + [(n, d1, buf1, T) for n in late])
        products = [(n, d1, buf1, None) for n in late]
        if has_next:
            next_queries = functools.partial(load_queries, i + 1)
            products += [(n, 0, buf0, next_queries if n == every[0] else None) for n in every]
        interleave(updates, products)
        row0 = pl.multiple_of(i * tq, tq)
        for c in range(n_chunks):
            ot = []
            for hh in range(2):
                acc = acc_scr[hh * n_chunks + c]
                ot.append(acc[:HEAD_DIM] / acc[HEAD_DIM:HEAD_DIM + 1])
            o_ref[0, pl.ds(row0 + c * Q_CHUNK, Q_CHUNK), :] = jnp.concatenate(ot, axis=0).T.astype(bf16)
        reset_stats()
        return carry

    load_queries(0)
    reset_stats()
    scores(0, *buf0, every)
    lax.fori_loop(0, n_tiles - 1, tile, 0)
    tile(n_tiles - 1, 0, has_next=False)


def _attn(qt, qat, k, ka, vt, *, tk=512):
    B, S, D = k.shape
    n_pairs = D // LANES
    tq = 2 * tk
    assert S % tq == 0 and tk % Q_CHUNK == 0 and LANES == 2 * HEAD_DIM
    n_streams = 2 * (tq // Q_CHUNK)
    tblk = pl.BlockSpec((1, LANES, S), lambda b, p: (b, p, 0))
    kblk = pl.BlockSpec((1, S, LANES), lambda b, p: (b, 0, p))
    return pl.pallas_call(
        functools.partial(_attn_kernel, tq=tq, tk=tk),
        grid=(B, n_pairs),
        in_specs=[tblk, tblk, kblk, kblk, tblk],
        out_specs=kblk,
        out_shape=jax.ShapeDtypeStruct((B, S, D), bf16),
        scratch_shapes=[
            pltpu.VMEM((S, 2 * LANES), bf16),
            pltpu.VMEM((2, V_ROWS, S), bf16),
            pltpu.VMEM((2, 2 * LANES, tq), bf16),
            pltpu.VMEM((n_streams, V_ROWS, Q_CHUNK), f32),
            pltpu.VMEM((n_streams, 1, Q_CHUNK), f32),
        ] + 2 * [pltpu.VMEM((n_streams, tk, Q_CHUNK), f32), pltpu.VMEM((n_streams, 1, Q_CHUNK), f32)],
        compiler_params=pltpu.CompilerParams(
            dimension_semantics=("parallel", "parallel"), vmem_limit_bytes=VMEM_LIMIT),
        name="attn",
    )(qt, qat, k, ka, vt)


def kernel(x, mix_norm, ffn_norm, pool_w, pool_scale, kv_norm, w_k, w_v, w_f, b_f, w_q, w_o,
           router_g, router_g_b, router_e, router_e_b, w_gate, w_up, w_down, final_norm):
    B, S, D = x.shape
    assert mix_norm.shape[0] == 2 and pool_w.shape[0] == 1 and w_q.shape[0] == 1

    def moe(h, layer, **kw):
        return _moe(h, ffn_norm[layer], router_g[layer], router_g_b[layer], router_e[layer],
                    router_e_b[layer], w_gate[layer], w_up[layer], w_down[layer], **kw)

    h = _mix0(x, mix_norm[0], pool_w[0], pool_scale[0])
    h = moe(h.reshape(B * S, D), 0)
    qt, qat, k, ka, vt = _kvq(h.reshape(B, S, D), kv_norm, mix_norm[1], w_k, w_v, w_q[0], w_f, b_f)
    o = _attn(qt, qat, k, ka, vt)
    out = moe(h, 1, attn=o.reshape(B * S, D), w_o=w_o[0], final_gain=final_norm)
    return out.reshape(B, S, D)
```
